```python
import math
import jax, jax.numpy as jnp
from jax import lax
import numpy as np

D_MODEL = 4096
BATCH = 4
SEQ = 4096
DEPTH = 2

ATTN_HEAD_DIM = 64
ATTN_Q_HEADS = 32
ATTN_KV_HEADS = 4
ATTN_GROUP = ATTN_Q_HEADS // ATTN_KV_HEADS
ATTN_Q_W = ATTN_Q_HEADS * ATTN_HEAD_DIM
ATTN_KV_W = ATTN_KV_HEADS * ATTN_HEAD_DIM
WINDOW = 128
ATTN_BLOCK = 128
ROT_DIM = ATTN_HEAD_DIM // 4
ROPE_THETA = 500000.0
SSD_INNER = 2048
SSD_HEAD_DIM = 64
SSD_HEADS = SSD_INNER // SSD_HEAD_DIM
SSD_GROUPS = 4
SSD_STATE = 128
SSD_CONV = 4
SSD_CHUNK = 128
SSD_CONV_CH = SSD_INNER + 2 * SSD_GROUPS * SSD_STATE
S5_WIDTH = 1536
S5_GROUP_SIZE = 16
S5_GROUPS = S5_WIDTH // S5_GROUP_SIZE
S5_STATE = 64
N_BRANCHES = 3
MIX_WIDTH = ATTN_Q_W + SSD_INNER + S5_WIDTH
IN_SIZES = (ATTN_Q_W, ATTN_KV_W, ATTN_KV_W, SSD_INNER, SSD_CONV_CH, SSD_HEADS, S5_WIDTH, N_BRANCHES * D_MODEL)
IN_COLS = ATTN_Q_W + 2 * ATTN_KV_W + SSD_INNER + SSD_CONV_CH + SSD_HEADS + S5_WIDTH + N_BRANCHES * D_MODEL
D_FF = 8192
N_EXPERTS = 8
TOP_K = 2
D_FF_EXPERT = 3072
N_DENSE = (DEPTH + 1) // 2
N_MOE = DEPTH // 2
RMS_EPS = 1e-6

kernel_name = 'hybrid_gated_swa_ssd_s5_moe_block'


def rms_norm(x, w):
    xf = x.astype(jnp.float32)
    y = xf * lax.rsqrt(jnp.mean(xf * xf, axis=-1, keepdims=True) + RMS_EPS)
    return (y * w.astype(jnp.float32)).astype(x.dtype)


def modulate(h, shift, scale):
    return h * (1.0 + scale[:, None, :]) + shift[:, None, :]


def rotary_tables(positions):
    inv_freq = ROPE_THETA ** (-jnp.arange(0, ROT_DIM, 2, dtype=jnp.float32) / ROT_DIM)
    ang = positions.astype(jnp.float32)[..., None] * inv_freq
    return jnp.cos(ang), jnp.sin(ang)


def apply_partial_rope(t, cos, sin):
    half = ROT_DIM // 2
    tf = t.astype(jnp.float32)
    r1, r2 = tf[..., :half], tf[..., half:ROT_DIM]
    c, s = cos[:, :, None, :], sin[:, :, None, :]
    out = jnp.concatenate([r1 * c - r2 * s, r2 * c + r1 * s, tf[..., ROT_DIM:]], axis=-1)
    return out.astype(t.dtype)


def sliding_window_attention(q, k, v, sinks):
    b, l = q.shape[:2]
    nb = l // ATTN_BLOCK
    qb = q.reshape(b, nb, ATTN_BLOCK, ATTN_KV_HEADS, ATTN_GROUP, ATTN_HEAD_DIM)

    def band(t):
        t = t.reshape(b, nb, ATTN_BLOCK, ATTN_KV_HEADS, ATTN_HEAD_DIM)
        prev = jnp.pad(t, ((0, 0), (1, 0), (0, 0), (0, 0), (0, 0)))[:, :-1]
        return jnp.concatenate([prev, t], axis=2)

    kb, vb = band(k), band(v)
    scale = ATTN_HEAD_DIM ** -0.5
    scores = jnp.einsum('bnqkgd,bnskd->bnkgqs', qb, kb).astype(jnp.float32) * scale
    qi = jnp.arange(ATTN_BLOCK)[:, None]
    sj = jnp.arange(2 * ATTN_BLOCK)[None, :]
    rel = qi + ATTN_BLOCK - sj
    valid = (rel >= 0) & (rel < WINDOW)
    not_pad = (jnp.arange(nb)[:, None, None] > 0) | (sj >= ATTN_BLOCK)[None]
    mask = valid[None] & not_pad
    scores = jnp.where(mask[None, :, None, None], scores, -jnp.inf)
    sink = sinks.astype(jnp.float32).reshape(ATTN_KV_HEADS, ATTN_GROUP)[None, None, :, :, None, None]
    sink = jnp.broadcast_to(sink, scores.shape[:-1] + (1,))
    probs = jax.nn.softmax(jnp.concatenate([scores, sink], axis=-1), axis=-1)[..., :-1]
    out = jnp.einsum('bnkgqs,bnskd->bnqkgd', probs.astype(vb.dtype), vb)
    return out.reshape(b, l, ATTN_Q_W)


def ssd_chunked(x, dt, a, bm, cm):
    b, l = x.shape[:2]
    nc = l // SSD_CHUNK
    r = SSD_HEADS // SSD_GROUPS
    x = x.reshape(b, nc, SSD_CHUNK, SSD_GROUPS, r, SSD_HEAD_DIM)
    dt = dt.reshape(b, nc, SSD_CHUNK, SSD_GROUPS, r)
    bm = bm.reshape(b, nc, SSD_CHUNK, SSD_GROUPS, SSD_STATE)
    cm = cm.reshape(b, nc, SSD_CHUNK, SSD_GROUPS, SSD_STATE)
    cs = jnp.cumsum(dt * a.reshape(SSD_GROUPS, r), axis=2)
    causal = jnp.tril(jnp.ones((SSD_CHUNK, SSD_CHUNK), dtype=bool))
    seg = cs[:, :, :, None] - cs[:, :, None, :]
    decay = jnp.exp(jnp.where(causal[:, :, None, None], seg, -jnp.inf))
    cb = jnp.einsum('bclgn,bcsgn->bclsg', cm, bm)
    w = cb[..., None] * decay * dt[:, :, None]
    y_diag = jnp.einsum('bclsgr,bcsgrp->bclgrp', w, x)
    decay_end = jnp.exp(cs[:, :, -1:] - cs)
    states = jnp.einsum('bcsgn,bcsgrp->bcgrpn', bm, (decay_end * dt)[..., None] * x)
    chunk_decay = jnp.exp(cs[:, :, -1])

    def step(carry, inp):
        st, dec = inp
        return carry * dec[..., None, None] + st, carry

    _, prev = lax.scan(step, jnp.zeros_like(states[:, 0]),
                       (jnp.moveaxis(states, 1, 0), jnp.moveaxis(chunk_decay, 1, 0)))
    prev = jnp.moveaxis(prev, 0, 1)
    y_off = jnp.einsum('bclgn,bcgrpn->bclgrp', cm, prev) * jnp.exp(cs)[..., None]
    return (y_diag + y_off).reshape(b, l, SSD_HEADS, SSD_HEAD_DIM)


def mamba2_mixer(z, xbc, dt_raw, conv_w, conv_b, dt_bias, a_log, d_skip, norm_w):
    b, l, _ = z.shape
    xbc = lax.conv_general_dilated(xbc, conv_w[:, None, :], (1,), [(SSD_CONV - 1, 0)],
                                   dimension_numbers=('NWC', 'WIO', 'NWC'),
                                   feature_group_count=SSD_CONV_CH)
    xbc = jax.nn.silu(xbc + conv_b).astype(jnp.float32)
    xs, bm, cm = jnp.split(xbc, [SSD_INNER, SSD_INNER + SSD_GROUPS * SSD_STATE], axis=-1)
    dt = jax.nn.softplus(dt_raw.astype(jnp.float32) + dt_bias.astype(jnp.float32))
    a = -jnp.exp(a_log.astype(jnp.float32))
    xh = xs.reshape(b, l, SSD_HEADS, SSD_HEAD_DIM)
    y = ssd_chunked(xh, dt, a, bm.reshape(b, l, SSD_GROUPS, SSD_STATE), cm.reshape(b, l, SSD_GROUPS, SSD_STATE))
    y = y + d_skip.astype(jnp.float32)[:, None] * xh
    y = y.reshape(b, l, SSD_INNER) * jax.nn.silu(z.astype(jnp.float32))
    yg = y.reshape(b, l, SSD_GROUPS, SSD_INNER // SSD_GROUPS)
    yg = yg * lax.rsqrt(jnp.mean(yg * yg, axis=-1, keepdims=True) + RMS_EPS)
    y = yg.reshape(b, l, SSD_INNER) * norm_w.astype(jnp.float32)
    return y.astype(z.dtype)


def complex_affine_combine(e1, e2):
    a1r, a1i, b1r, b1i = e1
    a2r, a2i, b2r, b2i = e2
    return (a2r * a1r - a2i * a1i, a2r * a1i + a2i * a1r,
            a2r * b1r - a2i * b1i + b2r, a2r * b1i + a2i * b1r + b2i)


def s5_mixer(u, lam_re, lam_im, log_step, b_re, b_im, c_re, c_im, d_skip, w_glu, b_glu):
    bsz, l, _ = u.shape
    uf = u.astype(jnp.float32).reshape(bsz, l, S5_GROUPS, S5_GROUP_SIZE)
    lr, li = lam_re.astype(jnp.float32), lam_im.astype(jnp.float32)
    step = jnp.exp(log_step.astype(jnp.float32))[:, None]
    mag = jnp.exp(lr * step)
    a_re, a_im = mag * jnp.cos(li * step), mag * jnp.sin(li * step)
    den = lr * lr + li * li
    coef_re = ((a_re - 1.0) * lr + a_im * li) / den
    coef_im = (a_im * lr - (a_re - 1.0) * li) / den
    br, bi = b_re.astype(jnp.float32), b_im.astype(jnp.float32)
    bbar_re = coef_re[..., None] * br - coef_im[..., None] * bi
    bbar_im = coef_re[..., None] * bi + coef_im[..., None] * br
    bu_re = jnp.einsum('blgi,gpi->blgp', uf, bbar_re)
    bu_im = jnp.einsum('blgi,gpi->blgp', uf, bbar_im)
    a_re_t = jnp.broadcast_to(a_re, (1, l, S5_GROUPS, S5_STATE))
    a_im_t = jnp.broadcast_to(a_im, (1, l, S5_GROUPS, S5_STATE))
    _, _, x_re, x_im = lax.associative_scan(complex_affine_combine, (a_re_t, a_im_t, bu_re, bu_im), axis=1)
    y = (jnp.einsum('blgp,gip->blgi', x_re, c_re.astype(jnp.float32))
         - jnp.einsum('blgp,gip->blgi', x_im, c_im.astype(jnp.float32)))
    y = y.reshape(bsz, l, S5_WIDTH) + d_skip.astype(jnp.float32) * u.astype(jnp.float32)
    y = jax.nn.gelu(y)
    y = y * jax.nn.sigmoid(y @ w_glu.astype(jnp.float32) + b_glu.astype(jnp.float32))
    return y.astype(u.dtype)


def hybrid_mixer(h, cos, sin, w_in, sinks, conv_w, conv_b, dt_bias, a_log, ssd_d, ssd_norm,
                 lam_re, lam_im, log_step, b_re, b_im, c_re, c_im, s5_d, w_glu, b_glu, w_branch, w_out):
    bsz, l, _ = h.shape
    proj = h @ w_in
    pts = []
    acc = 0
    for size in IN_SIZES[:-1]:
        acc += size
        pts.append(acc)
    q, k, v, z, xbc, dt_raw, u, gate_in = jnp.split(proj, pts, axis=-1)
    q = apply_partial_rope(q.reshape(bsz, l, ATTN_Q_HEADS, ATTN_HEAD_DIM), cos, sin)
    k = apply_partial_rope(k.reshape(bsz, l, ATTN_KV_HEADS, ATTN_HEAD_DIM), cos, sin)
    v = v.reshape(bsz, l, ATTN_KV_HEADS, ATTN_HEAD_DIM)
    o_attn = sliding_window_attention(q, k, v, sinks)
    o_ssd = mamba2_mixer(z, xbc, dt_raw, conv_w, conv_b, dt_bias, a_log, ssd_d, ssd_norm)
    o_s5 = s5_mixer(u, lam_re, lam_im, log_step, b_re, b_im, c_re, c_im, s5_d, w_glu, b_glu)
    gates = jax.nn.sigmoid(gate_in).reshape(bsz, l, N_BRANCHES, D_MODEL)
    wa, wb, wc = jnp.split(w_branch, [ATTN_Q_W, ATTN_Q_W + SSD_INNER], axis=0)
    merged = (gates[:, :, 0] * (o_attn @ wa) + gates[:, :, 1] * (o_ssd @ wb)
              + gates[:, :, 2] * (o_s5 @ wc))
    return merged @ w_out


def swiglu(h, w1, w3, w2):
    return (jax.nn.silu(h @ w1) * (h @ w3)) @ w2


def moe_swiglu(h, w_router, w1, w3, w2):
    b, l, d = h.shape
    t = h.reshape(b * l, d)
    logits = (t @ w_router).astype(jnp.float32)
    top_vals, top_idx = lax.top_k(logits, TOP_K)
    weights = jax.nn.softmax(top_vals, axis=-1)
    combine = jnp.sum(jax.nn.one_hot(top_idx, N_EXPERTS, dtype=jnp.float32) * weights[..., None], axis=1)
    out = jnp.zeros_like(t)
    for e in range(N_EXPERTS):
        ye = (jax.nn.silu(t @ w1[e]) * (t @ w3[e])) @ w2[e]
        out = out + combine[:, e:e + 1].astype(t.dtype) * ye
    return out.reshape(b, l, d)


def setup_inputs(seed: int = 0) -> dict:
    key = jax.random.key(seed)
    ks = iter(jax.random.split(key, 40))
    f32 = jnp.float32

    def nrm(shape, scale):
        return jax.random.normal(next(ks), shape, f32) * scale

    def gain(shape):
        return 1.0 + nrm(shape, 0.02)

    L = DEPTH
    x = nrm((BATCH, SEQ, D_MODEL), 1.0)
    c = nrm((BATCH, D_MODEL), 1.0)
    offs = jax.random.randint(next(ks), (BATCH, 1), 0, 4096, dtype=jnp.int32)
    positions = offs + jnp.arange(SEQ, dtype=jnp.int32)[None, :]
    w_mod = nrm((L, D_MODEL, 6 * D_MODEL), 0.5 * D_MODEL ** -0.5)
    b_mod = nrm((L, 6 * D_MODEL), 0.02)
    norm_mix_pre = gain((L, D_MODEL))
    norm_mix_post = gain((L, D_MODEL))
    norm_ffn_pre = gain((L, D_MODEL))
    norm_ffn_post = gain((L, D_MODEL))
    w_in = nrm((L, D_MODEL, IN_COLS), D_MODEL ** -0.5)
    attn_sinks = nrm((L, ATTN_Q_HEADS), 0.5)
    conv_w = nrm((L, SSD_CONV, SSD_CONV_CH), SSD_CONV ** -0.5)
    conv_b = nrm((L, SSD_CONV_CH), 0.02)
    dt0 = jnp.exp(jax.random.uniform(next(ks), (L, SSD_HEADS), f32, math.log(1e-3), math.log(1e-1)))
    dt_bias = dt0 + jnp.log(-jnp.expm1(-dt0))
    a_log = jnp.log(jax.random.uniform(next(ks), (L, SSD_HEADS), f32, 1.0, 16.0))
    ssd_d = gain((L, SSD_HEADS))
    ssd_norm = gain((L, SSD_INNER))
    s5_lam_re = -0.5 + nrm((L, S5_GROUPS, S5_STATE), 0.01)
    s5_lam_im = math.pi * jnp.arange(S5_STATE, dtype=f32) + nrm((L, S5_GROUPS, S5_STATE), 0.01)
    s5_log_step = jax.random.uniform(next(ks), (L, S5_GROUPS), f32, math.log(1e-3), math.log(1e-1))
    s5_b_re = nrm((L, S5_GROUPS, S5_STATE, S5_GROUP_SIZE), (2 * S5_GROUP_SIZE) ** -0.5)
    s5_b_im = nrm((L, S5_GROUPS, S5_STATE, S5_GROUP_SIZE), (2 * S5_GROUP_SIZE) ** -0.5)
    s5_c_re = nrm((L, S5_GROUPS, S5_GROUP_SIZE, S5_STATE), S5_STATE ** -0.5)
    s5_c_im = nrm((L, S5_GROUPS, S5_GROUP_SIZE, S5_STATE), S5_STATE ** -0.5)
    s5_d = nrm((L, S5_WIDTH), 1.0)
    s5_w_glu = nrm((L, S5_WIDTH, S5_WIDTH), S5_WIDTH ** -0.5)
    s5_b_glu = nrm((L, S5_WIDTH), 0.02)
    w_branch = nrm((L, MIX_WIDTH, D_MODEL), SSD_INNER ** -0.5)
    w_out = nrm((L, D_MODEL, D_MODEL), D_MODEL ** -0.5)
    ffn_w1 = nrm((N_DENSE, D_MODEL, D_FF), D_MODEL ** -0.5)
    ffn_w3 = nrm((N_DENSE, D_MODEL, D_FF), D_MODEL ** -0.5)
    ffn_w2 = nrm((N_DENSE, D_FF, D_MODEL), D_FF ** -0.5)
    w_router = nrm((N_MOE, D_MODEL, N_EXPERTS), D_MODEL ** -0.5)
    moe_w1 = nrm((N_MOE, N_EXPERTS, D_MODEL, D_FF_EXPERT), D_MODEL ** -0.5)
    moe_w3 = nrm((N_MOE, N_EXPERTS, D_MODEL, D_FF_EXPERT), D_MODEL ** -0.5)
    moe_w2 = nrm((N_MOE, N_EXPERTS, D_FF_EXPERT, D_MODEL), D_FF_EXPERT ** -0.5)
    return {'x': x, 'c': c, 'positions': positions, 'w_mod': w_mod, 'b_mod': b_mod,
            'norm_mix_pre': norm_mix_pre, 'norm_mix_post': norm_mix_post,
            'norm_ffn_pre': norm_ffn_pre, 'norm_ffn_post': norm_ffn_post,
            'w_in': w_in, 'attn_sinks': attn_sinks, 'conv_w': conv_w, 'conv_b': conv_b,
            'dt_bias': dt_bias, 'a_log': a_log, 'ssd_d': ssd_d, 'ssd_norm': ssd_norm,
            's5_lam_re': s5_lam_re, 's5_lam_im': s5_lam_im, 's5_log_step': s5_log_step,
            's5_b_re': s5_b_re, 's5_b_im': s5_b_im, 's5_c_re': s5_c_re, 's5_c_im': s5_c_im,
            's5_d': s5_d, 's5_w_glu': s5_w_glu, 's5_b_glu': s5_b_glu,
            'w_branch': w_branch, 'w_out': w_out,
            'ffn_w1': ffn_w1, 'ffn_w3': ffn_w3, 'ffn_w2': ffn_w2,
            'w_router': w_router, 'moe_w1': moe_w1, 'moe_w3': moe_w3, 'moe_w2': moe_w2}


def reference(x, c, positions, w_mod, b_mod, norm_mix_pre, norm_mix_post, norm_ffn_pre, norm_ffn_post,
              w_in, attn_sinks, conv_w, conv_b, dt_bias, a_log, ssd_d, ssd_norm,
              s5_lam_re, s5_lam_im, s5_log_step, s5_b_re, s5_b_im, s5_c_re, s5_c_im,
              s5_d, s5_w_glu, s5_b_glu, w_branch, w_out,
              ffn_w1, ffn_w3, ffn_w2, w_router, moe_w1, moe_w3, moe_w2):
    cos, sin = rotary_tables(positions)
    c_act = jax.nn.silu(c)
    for i in range(DEPTH):
        mod = c_act @ w_mod[i] + b_mod[i]
        sh1, sc1, g1, sh2, sc2, g2 = jnp.split(mod, 6, axis=-1)
        h = modulate(rms_norm(x, norm_mix_pre[i]), sh1, sc1)
        m = hybrid_mixer(h, cos, sin, w_in[i], attn_sinks[i], conv_w[i], conv_b[i], dt_bias[i], a_log[i],
                         ssd_d[i], ssd_norm[i], s5_lam_re[i], s5_lam_im[i], s5_log_step[i],
                         s5_b_re[i], s5_b_im[i], s5_c_re[i], s5_c_im[i], s5_d[i], s5_w_glu[i], s5_b_glu[i],
                         w_branch[i], w_out[i])
        x = x + g1[:, None, :] * rms_norm(m, norm_mix_post[i])
        h = modulate(rms_norm(x, norm_ffn_pre[i]), sh2, sc2)
        j = i // 2
        if i % 2 == 0:
            f = swiglu(h, ffn_w1[j], ffn_w3[j], ffn_w2[j])
        else:
            f = moe_swiglu(h, w_router[j], moe_w1[j], moe_w3[j], moe_w2[j])
        x = x + g2[:, None, :] * rms_norm(f, norm_ffn_post[i])
    return x
```

```python
import functools
import math

import jax
import jax.numpy as jnp
from jax import lax
from jax.experimental import pallas as pl
from jax.experimental.pallas import tpu as pltpu

BF = jnp.bfloat16
F32 = jnp.float32

HEAD_DIM = 64
Q_HEADS = 32
KV_HEADS = 4
Q_GROUP = Q_HEADS // KV_HEADS
Q_W = Q_HEADS * HEAD_DIM
KV_W = KV_HEADS * HEAD_DIM
ATTN_BLOCK = 128
ROT_DIM = 16
ROT_HALF = ROT_DIM // 2
ROPE_THETA = 500000.0
SSD_INNER = 2048
SSD_HEADS = 32
SSD_HEAD_DIM = 64
SSD_GROUPS = 4
SSD_STATE = 128
SSD_CONV = 4
SSD_CHUNK = 128
SSD_CONV_CH = SSD_INNER + 2 * SSD_GROUPS * SSD_STATE
S5_WIDTH = 1536
S5_GROUP_SIZE = 16
S5_GROUPS = 96
S5_STATE = 64
S5_CHUNK = 16
S5_GROUP_BLOCK = 8
N_EXPERTS = 8
RMS_EPS = 1e-6
LANES = 128
MIB = 1024 * 1024
NEG_BIG = -1e30


def _cparams(semantics, vmem_bytes):
    limit = int(min(max(vmem_bytes * 1.25 + 4 * MIB, 16 * MIB), 60 * MIB))
    return pltpu.CompilerParams(dimension_semantics=semantics, vmem_limit_bytes=limit)


def _pick(n, candidates):
    for c in candidates:
        if n % c == 0:
            return c
    raise ValueError(f"no tile in {candidates} divides {n}")


def _silu(x):
    return x * jax.nn.sigmoid(x)


def _mod_kernel(c_ref, w_ref, b_ref, o_ref):
    c = c_ref[...]
    a = _silu(c).astype(BF)
    o_ref[0] = jnp.dot(a, w_ref[0].astype(BF), preferred_element_type=F32) + b_ref[0]


def _modulation(c, w_mod, b_mod):
    depth, d, n = w_mod.shape
    bsz = c.shape[0]
    rows = 8
    c_pad = jnp.zeros((rows, d), F32).at[:bsz].set(c)
    tn = _pick(n, (512, 256, 128))
    out = pl.pallas_call(
        _mod_kernel,
        out_shape=jax.ShapeDtypeStruct((depth, rows, n), F32),
        grid=(depth, n // tn),
        in_specs=[pl.BlockSpec((rows, d), lambda l, j: (0, 0)),
                  pl.BlockSpec((1, d, tn), lambda l, j: (l, 0, j)),
                  pl.BlockSpec((1, 1, tn), lambda l, j: (l, 0, j))],
        out_specs=pl.BlockSpec((1, rows, tn), lambda l, j: (l, 0, j)),
        compiler_params=_cparams(("parallel", "arbitrary"), 2 * d * tn * 4 + d * tn * 2),
        name="adaln_modulation",
    )(c_pad, w_mod, b_mod.reshape(depth, 1, n))
    return out[:, :bsz]


def _rms(x, w):
    return x * lax.rsqrt(jnp.mean(x * x, axis=-1, keepdims=True) + RMS_EPS) * w


def _pre_norm_kernel(x_ref, w_ref, sh_ref, sc_ref, h_ref):
    h = _rms(x_ref[0], w_ref[...])
    h_ref[0] = (h * (1.0 + sc_ref[0]) + sh_ref[0]).astype(h_ref.dtype)


def _pre_norm(x, w, shift, scale):
    bsz, l, d = x.shape
    tl = _pick(l, (256, 128))
    vec = pl.BlockSpec((1, 1, d), lambda b, i: (b, 0, 0))
    return pl.pallas_call(
        _pre_norm_kernel,
        out_shape=jax.ShapeDtypeStruct((bsz, l, d), BF),
        grid=(bsz, l // tl),
        in_specs=[pl.BlockSpec((1, tl, d), lambda b, i: (b, i, 0)),
                  pl.BlockSpec((1, d), lambda b, i: (0, 0)), vec, vec],
        out_specs=pl.BlockSpec((1, tl, d), lambda b, i: (b, i, 0)),
        compiler_params=_cparams(("parallel", "parallel"), 2 * tl * d * 6 + 2 * tl * d * 4),
        name="pre_norm",
    )(x, w.reshape(1, d), shift.reshape(bsz, 1, d), scale.reshape(bsz, 1, d))


def _post_norm_kernel(m_ref, x_ref, g_ref, wpost_ref, wpre_ref, sh_ref, sc_ref, xo_ref, h_ref):
    xn = x_ref[0] + g_ref[0] * _rms(m_ref[0].astype(F32), wpost_ref[...])
    xo_ref[0] = xn
    h = _rms(xn, wpre_ref[...])
    h_ref[0] = (h * (1.0 + sc_ref[0]) + sh_ref[0]).astype(h_ref.dtype)


def _post_norm_last_kernel(m_ref, x_ref, g_ref, wpost_ref, xo_ref):
    xo_ref[0] = x_ref[0] + g_ref[0] * _rms(m_ref[0].astype(F32), wpost_ref[...])


def _post_norm(m, x, gate, w_post, w_pre=None, shift=None, scale=None):
    bsz, l, d = x.shape
    tl = _pick(l, (256, 128))
    row = pl.BlockSpec((1, tl, d), lambda b, i: (b, i, 0))
    vec = pl.BlockSpec((1, 1, d), lambda b, i: (b, 0, 0))
    par = pl.BlockSpec((1, d), lambda b, i: (0, 0))
    m = m.reshape(bsz, l, d)
    gate = gate.reshape(bsz, 1, d)
    cp = _cparams(("parallel", "parallel"), 2 * tl * d * (2 + 4 + 4 + 2) + 3 * tl * d * 4)
    if w_pre is None:
        return pl.pallas_call(
            _post_norm_last_kernel,
            out_shape=jax.ShapeDtypeStruct((bsz, l, d), F32),
            grid=(bsz, l // tl),
            in_specs=[row, row, vec, par],
            out_specs=row,
            compiler_params=cp,
            name="post_norm_last",
        )(m, x, gate, w_post.reshape(1, d)), None
    return pl.pallas_call(
        _post_norm_kernel,
        out_shape=(jax.ShapeDtypeStruct((bsz, l, d), F32), jax.ShapeDtypeStruct((bsz, l, d), BF)),
        grid=(bsz, l // tl),
        in_specs=[row, row, vec, par, par, vec, vec],
        out_specs=(row, row),
        compiler_params=cp,
        name="post_norm",
    )(m, x, gate, w_post.reshape(1, d), w_pre.reshape(1, d), shift.reshape(bsz, 1, d),
      scale.reshape(bsz, 1, d))


def _mm_call(body, a_list, w_list, extras, extra_specs, *, n, tm, tn, out_dtype, name):
    m = a_list[0].shape[0]
    tm = min(tm, m)
    assert m % tm == 0 and n % tn == 0
    a_specs = [pl.BlockSpec((tm, a.shape[1]), lambda i, j: (i, 0)) for a in a_list]
    w_specs = [pl.BlockSpec((w.shape[0], tn), lambda i, j: (0, j)) for w in w_list]
    vmem = sum(2 * tm * a.shape[1] * a.dtype.itemsize for a in a_list)
    vmem += sum(2 * w.shape[0] * tn * w.dtype.itemsize for w in w_list)
    vmem += 2 * tm * tn * 4 * (1 + len(extras)) + 3 * tm * tn * 4
    return pl.pallas_call(
        body,
        out_shape=jax.ShapeDtypeStruct((m, n), out_dtype),
        grid=(m // tm, n // tn),
        in_specs=a_specs + w_specs + list(extra_specs(tm, tn)),
        out_specs=pl.BlockSpec((tm, tn), lambda i, j: (i, j)),
        compiler_params=_cparams(("parallel", "arbitrary"), vmem),
        name=name,
    )(*a_list, *w_list, *extras)


def _dot(a, w):
    return jnp.dot(a, w, preferred_element_type=F32)


def _plain_body(a_ref, w_ref, o_ref):
    o_ref[...] = _dot(a_ref[...], w_ref[...]).astype(o_ref.dtype)


def _sigmoid_body(a_ref, w_ref, o_ref):
    o_ref[...] = jax.nn.sigmoid(_dot(a_ref[...], w_ref[...])).astype(o_ref.dtype)


def _matmul(a, w, *, tn=None, tm=1024, out_dtype=BF, body=_plain_body, name="matmul"):
    n = w.shape[1]
    tn = tn or _pick(n, (512, 256, 128))
    return _mm_call(body, [a], [w], [], lambda tm_, tn_: [], n=n, tm=tm, tn=tn,
                    out_dtype=out_dtype, name=name)


def _qkv_body(n_q_tiles, n_rope_tiles, a_ref, w_ref, cos_ref, sa_ref, sb_ref, o_ref):
    j = pl.program_id(1)
    acc = _dot(a_ref[...], w_ref[...])
    tn = acc.shape[1]

    def rope(t):
        cols = []
        for c in range(tn // LANES):
            r = t[:, c * LANES:(c + 1) * LANES]
            cols.append(r * cos_ref[...] + pltpu.roll(r, ROT_HALF, axis=1) * sa_ref[...]
                        + pltpu.roll(r, LANES - ROT_HALF, axis=1) * sb_ref[...])
        return jnp.concatenate(cols, axis=1)

    @pl.when(j < n_q_tiles)
    def _():
        o_ref[...] = (rope(acc) * (HEAD_DIM ** -0.5)).astype(o_ref.dtype)

    @pl.when(jnp.logical_and(j >= n_q_tiles, j < n_rope_tiles))
    def _():
        o_ref[...] = rope(acc).astype(o_ref.dtype)

    @pl.when(j >= n_rope_tiles)
    def _():
        o_ref[...] = acc.astype(o_ref.dtype)


def _qkv_proj(h, w_qkv, cos_t, sa_t, sb_t):
    tn = 256
    n = w_qkv.shape[1]
    body = functools.partial(_qkv_body, Q_W // tn, (Q_W + KV_W) // tn)
    tab = lambda tm_, tn_: [pl.BlockSpec((tm_, LANES), lambda i, j: (i, 0))] * 3
    return _mm_call(body, [h], [w_qkv], [cos_t, sa_t, sb_t], tab, n=n, tm=1024, tn=tn,
                    out_dtype=BF, name="qkv_proj_rope")


def _swiglu_body(a_ref, w1_ref, w3_ref, o_ref):
    a = a_ref[...]
    o_ref[...] = (_silu(_dot(a, w1_ref[...])) * _dot(a, w3_ref[...])).astype(o_ref.dtype)


def _swiglu_up(h, w1, w3):
    n = w1.shape[1]
    tn = _pick(n, (512, 256, 128))
    return _mm_call(_swiglu_body, [h], [w1, w3], [], lambda tm_, tn_: [], n=n, tm=1024, tn=tn,
                    out_dtype=BF, name="swiglu_up")


def _merge_body(oa_ref, ob_ref, oc_ref, wa_ref, wb_ref, wc_ref, ga_ref, gb_ref, gc_ref, o_ref):
    acc = ga_ref[...].astype(F32) * _dot(oa_ref[...], wa_ref[...])
    acc += gb_ref[...].astype(F32) * _dot(ob_ref[...], wb_ref[...])
    acc += gc_ref[...].astype(F32) * _dot(oc_ref[...], wc_ref[...])
    o_ref[...] = acc.astype(o_ref.dtype)


def _branch_merge(o_attn, o_ssd, o_s5, wa, wb, wc, gates):
    d = wa.shape[1]
    tn = _pick(d, (512, 256, 128))
    nt = d // tn

    def gate_specs(tm_, tn_):
        return [pl.BlockSpec((tm_, tn_), functools.partial(lambda br, i, j: (i, br * nt + j), br))
                for br in range(3)]

    return _mm_call(_merge_body, [o_attn, o_ssd, o_s5], [wa, wb, wc], [gates, gates, gates],
                    gate_specs, n=d, tm=512, tn=tn, out_dtype=BF, name="branch_merge")


def _attn_kernel(sink_ref, q_ref, kc_ref, kp_ref, vc_ref, vp_ref, o_ref):
    n = pl.program_id(1)
    qi = lax.broadcasted_iota(jnp.int32, (ATTN_BLOCK, 2 * ATTN_BLOCK), 0)
    sj = lax.broadcasted_iota(jnp.int32, (ATTN_BLOCK, 2 * ATTN_BLOCK), 1)
    rel = qi + ATTN_BLOCK - sj
    valid = (rel >= 0) & (rel < ATTN_BLOCK) & ((sj >= ATTN_BLOCK) | (n > 0))
    q = q_ref[0]
    kc, kp, vc, vp = kc_ref[0], kp_ref[0], vc_ref[0], vp_ref[0]
    outs = []
    for g in range(KV_HEADS):
        sl = slice(g * HEAD_DIM, (g + 1) * HEAD_DIM)
        kb = jnp.concatenate([kp[:, sl], kc[:, sl]], axis=0)
        vb = jnp.concatenate([vp[:, sl], vc[:, sl]], axis=0)
        for r in range(Q_GROUP):
            hd = g * Q_GROUP + r
            qh = q[:, hd * HEAD_DIM:(hd + 1) * HEAD_DIM]
            s = lax.dot_general(qh, kb, (((1,), (1,)), ((), ())), preferred_element_type=F32)
            s = jnp.where(valid, s, NEG_BIG)
            sink = sink_ref[hd]
            mx = jnp.maximum(jnp.max(s, axis=-1, keepdims=True), sink)
            p = jnp.exp(s - mx)
            den = jnp.sum(p, axis=-1, keepdims=True) + jnp.exp(sink - mx)
            o = _dot(p.astype(BF), vb)
            outs.append(o / den)
    o_ref[0] = jnp.concatenate(outs, axis=1).astype(o_ref.dtype)


def _attention(qkv, sinks, bsz, l):
    nb = l // ATTN_BLOCK
    qkv = qkv.reshape(bsz, l, Q_W + 2 * KV_W)
    kcol, vcol = Q_W // KV_W, Q_W // KV_W + 1
    cur = lambda col: pl.BlockSpec((1, ATTN_BLOCK, KV_W), lambda b, n: (b, n, col))
    prev = lambda col: pl.BlockSpec((1, ATTN_BLOCK, KV_W), lambda b, n: (b, jnp.maximum(n - 1, 0), col))
    out = pl.pallas_call(
        _attn_kernel,
        out_shape=jax.ShapeDtypeStruct((bsz, l, Q_W), BF),
        grid=(bsz, nb),
        in_specs=[pl.BlockSpec(memory_space=pltpu.SMEM),
                  pl.BlockSpec((1, ATTN_BLOCK, Q_W), lambda b, n: (b, n, 0)),
                  cur(kcol), prev(kcol), cur(vcol), prev(vcol)],
        out_specs=pl.BlockSpec((1, ATTN_BLOCK, Q_W), lambda b, n: (b, n, 0)),
        compiler_params=_cparams(("parallel", "parallel"), 8 * MIB),
        name="sliding_window_attention",
    )(sinks.astype(F32), qkv, qkv, qkv, qkv, qkv)
    return out.reshape(bsz * l, Q_W)


def _split_dot(f, e):
    hi = f.astype(BF)
    lo = (f - hi.astype(F32)).astype(BF)
    return _dot(hi, e) + _dot(lo, e)


def _ssd_kernel(z_ref, xbc_ref, dt_ref, cw_ref, cb_ref, dtb_ref, alog_ref, dsk_ref, nw_ref,
                o_ref, ext_ref, state_ref):
    c = pl.program_id(1)
    q = SSD_CHUNK
    halo = 8

    @pl.when(c == 0)
    def _():
        ext_ref[0:halo, :] = jnp.zeros((halo, SSD_CONV_CH), F32)
        state_ref[...] = jnp.zeros_like(state_ref)

    ext_ref[halo:halo + q, :] = xbc_ref[0].astype(F32)
    conv = cb_ref[...] + cw_ref[SSD_CONV - 1:SSD_CONV, :] * ext_ref[halo:halo + q, :]
    for j in range(SSD_CONV - 1):
        off = halo - (SSD_CONV - 1) + j
        conv = conv + cw_ref[j:j + 1, :] * ext_ref[off:off + q, :]
    tail = ext_ref[q:q + halo, :]
    ext_ref[0:halo, :] = tail
    xbc = _silu(conv)
    xs = xbc[:, :SSD_INNER]
    xs_bf = xs.astype(BF)
    bm = xbc[:, SSD_INNER:SSD_INNER + SSD_GROUPS * SSD_STATE].astype(BF)
    cm = xbc[:, SSD_INNER + SSD_GROUPS * SSD_STATE:].astype(BF)

    dt = jax.nn.softplus(dt_ref[0] + dtb_ref[...])
    a = -jnp.exp(alog_ref[...])
    cs = dt * a
    row = lax.broadcasted_iota(jnp.int32, (q, LANES), 0)
    sh = 1
    while sh < q:
        cs = cs + jnp.where(row >= sh, pltpu.roll(cs, sh, axis=0), 0.0)
        sh *= 2
    cs_last = cs[q - 1:q, :]
    ecs = jnp.exp(cs)
    dte = jnp.exp(cs_last - cs) * dt
    cs_t = cs.T
    dt_t = dt.T

    hl = lax.broadcasted_iota(jnp.int32, (LANES, SSD_INNER), 0)
    hc = lax.broadcasted_iota(jnp.int32, (LANES, SSD_INNER), 1)
    expand = jnp.where(hl == hc // SSD_HEAD_DIM, 1.0, 0.0).astype(BF)
    ecs_x = _split_dot(ecs, expand)
    dte_x = _split_dot(dte, expand)

    li = lax.broadcasted_iota(jnp.int32, (q, q), 0)
    si = lax.broadcasted_iota(jnp.int32, (q, q), 1)
    causal = li >= si
    heads_per_group = SSD_HEADS // SSD_GROUPS
    gw = heads_per_group * SSD_HEAD_DIM
    y_parts = []
    for g in range(SSD_GROUPS):
        ns = slice(g * SSD_STATE, (g + 1) * SSD_STATE)
        gs = slice(g * gw, (g + 1) * gw)
        bm_g, cm_g = bm[:, ns], cm[:, ns]
        cb = lax.dot_general(cm_g, bm_g, (((1,), (1,)), ((), ())), preferred_element_type=F32)
        for r in range(heads_per_group):
            hd = g * heads_per_group + r
            seg = cs[:, hd:hd + 1] - cs_t[hd:hd + 1, :]
            decay = jnp.exp(jnp.where(causal, seg, NEG_BIG))
            w = (cb * decay * dt_t[hd:hd + 1, :]).astype(BF)
            y_parts.append(_dot(w, xs_bf[:, hd * SSD_HEAD_DIM:(hd + 1) * SSD_HEAD_DIM]))
    y = jnp.concatenate(y_parts, axis=1)
    xw = (xs * dte_x).astype(BF)
    for g in range(SSD_GROUPS):
        ns = slice(g * SSD_STATE, (g + 1) * SSD_STATE)
        gs = slice(g * gw, (g + 1) * gw)
        prev = state_ref[:, gs]
        y_off = _dot(cm[:, ns], prev.astype(BF)) * ecs_x[:, gs]
        upd = lax.dot_general(bm[:, ns], xw[:, gs], (((0,), (0,)), ((), ())),
                              preferred_element_type=F32)
        state_ref[:, gs] = prev * ecs_x[q - 1:q, gs] + upd
        yg = y[:, gs] + y_off + dsk_ref[:, gs] * xs[:, gs]
        yg = yg * _silu(z_ref[0, :, gs].astype(F32))
        yg = yg * lax.rsqrt(jnp.mean(yg * yg, axis=-1, keepdims=True) + RMS_EPS)
        o_ref[0, :, gs] = (yg * nw_ref[:, gs]).astype(o_ref.dtype)


def _ssd(z, xbc, dt_raw, conv_w, conv_b, dt_bias, a_log, d_skip, norm_w, bsz, l):
    q = SSD_CHUNK
    pad = LANES - SSD_HEADS
    row = lambda w: pl.BlockSpec((1, q, w), lambda b, c: (b, c, 0))
    par = lambda r, w: pl.BlockSpec((r, w), lambda b, c: (0, 0))
    out = pl.pallas_call(
        _ssd_kernel,
        out_shape=jax.ShapeDtypeStruct((bsz, l, SSD_INNER), BF),
        grid=(bsz, l // q),
        in_specs=[row(SSD_INNER), row(SSD_CONV_CH), row(LANES),
                  par(SSD_CONV, SSD_CONV_CH), par(1, SSD_CONV_CH), par(1, LANES), par(1, LANES),
                  par(1, SSD_INNER), par(1, SSD_INNER)],
        out_specs=row(SSD_INNER),
        scratch_shapes=[pltpu.VMEM((q + 8, SSD_CONV_CH), F32), pltpu.VMEM((SSD_STATE, SSD_INNER), F32)],
        compiler_params=_cparams(("parallel", "arbitrary"), 24 * MIB),
        name="ssd_chunked",
    )(z.reshape(bsz, l, SSD_INNER), xbc.reshape(bsz, l, SSD_CONV_CH), dt_raw.reshape(bsz, l, LANES),
      conv_w, conv_b.reshape(1, -1), jnp.pad(dt_bias, (0, pad)).reshape(1, LANES),
      jnp.pad(a_log, (0, pad)).reshape(1, LANES),
      jnp.repeat(d_skip, SSD_HEAD_DIM).reshape(1, SSD_INNER), norm_w.reshape(1, SSD_INNER))
    return out.reshape(bsz * l, SSD_INNER)


def _s5_operators(lam_re, lam_im, log_step, b_re, b_im, c_re, c_im, n_chunks):
    hp = lax.Precision.HIGHEST
    qn = S5_CHUNK
    lr, li = lam_re.astype(F32), lam_im.astype(F32)
    step = jnp.exp(log_step.astype(F32))[:, None]
    mag = jnp.exp(lr * step)
    a_re, a_im = mag * jnp.cos(li * step), mag * jnp.sin(li * step)
    den = lr * lr + li * li
    coef_re = ((a_re - 1.0) * lr + a_im * li) / den
    coef_im = (a_im * lr - (a_re - 1.0) * li) / den
    br, bi = b_re.astype(F32), b_im.astype(F32)
    bb_re = coef_re[..., None] * br - coef_im[..., None] * bi
    bb_im = coef_re[..., None] * bi + coef_im[..., None] * br

    def power(d):
        d = d.astype(F32)[None, :, None]
        m = jnp.exp(lr[:, None, :] * step[:, None, :] * d)
        ang = li[:, None, :] * step[:, None, :] * d
        return m * jnp.cos(ang), m * jnp.sin(ang)

    p_re, p_im = power(jnp.arange(qn + 1))
    cr, ci = c_re.astype(F32), c_im.astype(F32)
    m_re = cr[:, None] * p_re[:, :, None, :] - ci[:, None] * p_im[:, :, None, :]
    m_im = cr[:, None] * p_im[:, :, None, :] + ci[:, None] * p_re[:, :, None, :]
    kern = (jnp.einsum('gdip,gpj->gdij', m_re, bb_re, precision=hp)
            - jnp.einsum('gdip,gpj->gdij', m_im, bb_im, precision=hp))
    s_idx = jnp.arange(qn)[:, None]
    t_idx = jnp.arange(qn)[None, :]
    lag = t_idx - s_idx
    toep = kern[:, jnp.clip(lag, 0, qn)]
    toep = jnp.where((lag >= 0)[None, :, :, None, None], toep, 0.0)
    toep = toep.transpose(0, 1, 4, 2, 3).reshape(S5_GROUPS, qn * S5_GROUP_SIZE, qn * S5_GROUP_SIZE)
    e_re, e_im = p_re[:, qn - 1 - jnp.arange(qn)], p_im[:, qn - 1 - jnp.arange(qn)]
    be_re = e_re[:, :, None, :] * bb_re.transpose(0, 2, 1)[:, None] - e_im[:, :, None, :] * bb_im.transpose(0, 2, 1)[:, None]
    be_im = e_re[:, :, None, :] * bb_im.transpose(0, 2, 1)[:, None] + e_im[:, :, None, :] * bb_re.transpose(0, 2, 1)[:, None]
    b_end = jnp.concatenate([be_re, be_im], axis=-1).reshape(S5_GROUPS, qn * S5_GROUP_SIZE, 2 * S5_STATE)
    ci_re = m_re[:, 1:].transpose(0, 3, 1, 2)
    ci_im = -m_im[:, 1:].transpose(0, 3, 1, 2)
    c_in = jnp.concatenate([ci_re, ci_im], axis=1).reshape(S5_GROUPS, 2 * S5_STATE, qn * S5_GROUP_SIZE)
    n_steps = max(1, (n_chunks - 1).bit_length())
    s_re, s_im = power(qn * (2 ** jnp.arange(n_steps)))
    pw1 = jnp.concatenate([s_re, s_re], axis=-1)
    pw2 = jnp.concatenate([-s_im, s_im], axis=-1)
    return toep.astype(BF), b_end.astype(BF), c_in.astype(BF), pw1, pw2


def _s5_kernel(u_ref, toep_ref, bend_ref, cin_ref, pw1_ref, pw2_ref, y_ref):
    n_chunks = u_ref.shape[2]
    n_steps = pw1_ref.shape[1]
    row = lax.broadcasted_iota(jnp.int32, (n_chunks, 2 * S5_STATE), 0)
    for g in range(u_ref.shape[0]):
        u = u_ref[g, 0]
        x = _dot(u, bend_ref[g])
        for k in range(n_steps):
            sh = 1 << k
            xs = jnp.where(row >= sh, pltpu.roll(x, sh, axis=0), 0.0)
            x = x + pw1_ref[g, k:k + 1, :] * xs + pw2_ref[g, k:k + 1, :] * pltpu.roll(xs, S5_STATE, axis=1)
        x_in = jnp.where(row >= 1, pltpu.roll(x, 1, axis=0), 0.0)
        y = _dot(u, toep_ref[g]) + _dot(x_in.astype(BF), cin_ref[g])
        y_ref[g, 0] = y.astype(y_ref.dtype)


def _s5_scan(u, ops, bsz, l):
    toep, b_end, c_in, pw1, pw2 = ops
    qn, gs, ng, gb = S5_CHUNK, S5_GROUP_SIZE, S5_GROUPS, S5_GROUP_BLOCK
    nc = l // qn
    w = qn * gs
    ug = u.reshape(bsz, nc, qn, ng, gs).transpose(3, 0, 1, 2, 4).reshape(ng, bsz, nc, w)
    wspec = lambda r, c: pl.BlockSpec((gb, r, c), lambda g, b: (g, 0, 0))
    y = pl.pallas_call(
        _s5_kernel,
        out_shape=jax.ShapeDtypeStruct((ng, bsz, nc, w), BF),
        grid=(ng // gb, bsz),
        in_specs=[pl.BlockSpec((gb, 1, nc, w), lambda g, b: (g, b, 0, 0)),
                  wspec(w, w), wspec(w, 2 * S5_STATE), wspec(2 * S5_STATE, w),
                  wspec(pw1.shape[1], 2 * S5_STATE), wspec(pw1.shape[1], 2 * S5_STATE)],
        out_specs=pl.BlockSpec((gb, 1, nc, w), lambda g, b: (g, b, 0, 0)),
        compiler_params=_cparams(("parallel", "arbitrary"), 16 * MIB),
        name="s5_chunked_scan",
    )(ug, toep, b_end, c_in, pw1, pw2)
    return y.reshape(ng, bsz, nc, qn, gs).transpose(1, 2, 3, 0, 4).reshape(bsz * l, S5_WIDTH)


def _s5_glu_kernel(y_ref, u_ref, d_ref, w_ref, b_ref, o_ref):
    y = y_ref[...].astype(F32) + d_ref[...] * u_ref[...].astype(F32)
    v = jax.nn.gelu(y, approximate=True)
    o_ref[...] = (v * jax.nn.sigmoid(_dot(v.astype(BF), w_ref[...]) + b_ref[...])).astype(o_ref.dtype)


def _s5_glu(y, u, d_skip, w_glu, b_glu):
    m, w = y.shape
    tm = _pick(m, (512, 256, 128))
    row = pl.BlockSpec((tm, w), lambda i: (i, 0))
    par = pl.BlockSpec((1, w), lambda i: (0, 0))
    return pl.pallas_call(
        _s5_glu_kernel,
        out_shape=jax.ShapeDtypeStruct((m, w), BF),
        grid=(m // tm,),
        in_specs=[row, row, par, pl.BlockSpec((w, w), lambda i: (0, 0)), par],
        out_specs=row,
        compiler_params=_cparams(("parallel",), 6 * tm * w * 2 + 2 * w * w * 2 + 4 * tm * w * 4),
        name="s5_gelu_glu",
    )(y, u, d_skip.reshape(1, w), w_glu, b_glu.reshape(1, w))


def _router_kernel(h_ref, w_ref, o_ref):
    logits = _dot(h_ref[...], w_ref[...])
    lane = lax.broadcasted_iota(jnp.int32, logits.shape, 1)
    lg = jnp.where(lane < N_EXPERTS, logits, -jnp.inf)
    m1 = jnp.max(lg, axis=-1, keepdims=True)
    i1 = jnp.min(jnp.where(lg == m1, lane, LANES), axis=-1, keepdims=True)
    lg2 = jnp.where(lane == i1, -jnp.inf, lg)
    m2 = jnp.max(lg2, axis=-1, keepdims=True)
    i2 = jnp.min(jnp.where(lg2 == m2, lane, LANES), axis=-1, keepdims=True)
    e = jnp.exp(m2 - m1)
    w1 = 1.0 / (1.0 + e)
    o_ref[...] = jnp.where(lane == i1, w1, 0.0) + jnp.where(lane == i2, e * w1, 0.0)


def _router(h, w_router):
    m, d = h.shape
    tm = _pick(m, (1024, 512, 256, 128))
    w_pad = jnp.zeros((d, LANES), BF).at[:, :N_EXPERTS].set(w_router.astype(BF))
    return pl.pallas_call(
        _router_kernel,
        out_shape=jax.ShapeDtypeStruct((m, LANES), F32),
        grid=(m // tm,),
        in_specs=[pl.BlockSpec((tm, d), lambda i: (i, 0)), pl.BlockSpec((d, LANES), lambda i: (0, 0))],
        out_specs=pl.BlockSpec((tm, LANES), lambda i: (i, 0)),
        compiler_params=_cparams(("parallel",), 2 * tm * d * 2 + 2 * d * LANES * 2 + 6 * tm * LANES * 4),
        name="moe_router",
    )(h, w_pad)


def _moe_up_body(tiles_per_expert, a_ref, w1_ref, w3_ref, comb_ref, o_ref):
    e = pl.program_id(1) // tiles_per_expert
    a = a_ref[...]
    comb = comb_ref[...]
    lane = lax.broadcasted_iota(jnp.int32, comb.shape, 1)
    ce = jnp.sum(jnp.where(lane == e, comb, 0.0), axis=-1, keepdims=True)
    act = _silu(_dot(a, w1_ref[0])) * _dot(a, w3_ref[0])
    o_ref[...] = (act * ce).astype(o_ref.dtype)


def _moe_down_body(a_ref, w_ref, o_ref, acc_ref):
    e = pl.program_id(2)

    @pl.when(e == 0)
    def _():
        acc_ref[...] = jnp.zeros_like(acc_ref)

    acc_ref[...] += _dot(a_ref[...], w_ref[0])

    @pl.when(e == pl.num_programs(2) - 1)
    def _():
        o_ref[...] = acc_ref[...].astype(o_ref.dtype)


def _moe(h, w_router, w1, w3, w2):
    m, d = h.shape
    ne, _, f = w1.shape
    comb = _router(h, w_router)
    tm = min(1024, m)
    tn = _pick(f, (512, 256, 128))
    tpe = f // tn
    wspec = pl.BlockSpec((1, d, tn), lambda i, j: (j // tpe, 0, j % tpe))
    act = pl.pallas_call(
        functools.partial(_moe_up_body, tpe),
        out_shape=jax.ShapeDtypeStruct((m, ne * f), BF),
        grid=(m // tm, ne * tpe),
        in_specs=[pl.BlockSpec((tm, d), lambda i, j: (i, 0)), wspec, wspec,
                  pl.BlockSpec((tm, LANES), lambda i, j: (i, 0))],
        out_specs=pl.BlockSpec((tm, tn), lambda i, j: (i, j)),
        compiler_params=_cparams(("parallel", "arbitrary"), 2 * tm * d * 2 + 4 * d * tn * 2 + 6 * tm * tn * 4),
        name="moe_up",
    )(h, w1, w3, comb)
    tn2 = _pick(d, (512, 256, 128))
    return pl.pallas_call(
        _moe_down_body,
        out_shape=jax.ShapeDtypeStruct((m, d), BF),
        grid=(m // tm, d // tn2, ne),
        in_specs=[pl.BlockSpec((tm, f), lambda i, j, e: (i, e)),
                  pl.BlockSpec((1, f, tn2), lambda i, j, e: (e, 0, j))],
        out_specs=pl.BlockSpec((tm, tn2), lambda i, j, e: (i, j)),
        scratch_shapes=[pltpu.VMEM((tm, tn2), F32)],
        compiler_params=_cparams(("parallel", "parallel", "arbitrary"),
                                 2 * tm * f * 2 + 2 * f * tn2 * 2 + 5 * tm * tn2 * 4),
        name="moe_down",
    )(act, w2)


def _rope_tables(positions):
    inv_freq = ROPE_THETA ** (-jnp.arange(0, ROT_DIM, 2, dtype=F32) / ROT_DIM)
    ang = positions.astype(F32)[..., None] * inv_freq
    cos, sin = jnp.cos(ang), jnp.sin(ang)
    rest = HEAD_DIM - ROT_DIM
    ones = jnp.ones(ang.shape[:-1] + (rest,), F32)
    zeros = jnp.zeros(ang.shape[:-1] + (rest,), F32)
    z8 = jnp.zeros_like(sin)
    reps = LANES // HEAD_DIM
    flat = lambda t: jnp.tile(t, (1, 1, reps)).reshape(-1, LANES)
    cos_t = flat(jnp.concatenate([cos, cos, ones], axis=-1))
    sa_t = flat(jnp.concatenate([z8, sin, zeros], axis=-1))
    sb_t = flat(jnp.concatenate([-sin, z8, zeros], axis=-1))
    return cos_t, sa_t, sb_t


def _hybrid_mixer(h, rope, bsz, l, w_in, sinks, conv_w, conv_b, dt_bias, a_log, ssd_d, ssd_norm,
                  lam_re, lam_im, log_step, b_re, b_im, c_re, c_im, s5_d, w_glu, b_glu, w_branch, w_out):
    d = h.shape[1]
    o0 = Q_W + 2 * KV_W
    o1 = o0 + SSD_INNER
    o2 = o1 + SSD_CONV_CH
    o3 = o2 + SSD_HEADS
    o4 = o3 + S5_WIDTH
    cast = lambda w: w.astype(BF)
    qkv = _qkv_proj(h, cast(w_in[:, :o0]), *rope)
    z = _matmul(h, cast(w_in[:, o0:o1]), name="z_proj")
    xbc = _matmul(h, cast(w_in[:, o1:o2]), name="xbc_proj")
    w_dt = jnp.zeros((d, LANES), BF).at[:, :SSD_HEADS].set(cast(w_in[:, o2:o3]))
    dt_raw = _matmul(h, w_dt, tn=LANES, out_dtype=F32, name="dt_proj")
    u = _matmul(h, cast(w_in[:, o3:o4]), name="u_proj")
    gates = _matmul(h, cast(w_in[:, o4:]), body=_sigmoid_body, name="gate_proj")

    o_attn = _attention(qkv, sinks, bsz, l)
    o_ssd = _ssd(z, xbc, dt_raw, conv_w, conv_b, dt_bias, a_log, ssd_d, ssd_norm, bsz, l)
    ops = _s5_operators(lam_re, lam_im, log_step, b_re, b_im, c_re, c_im, l // S5_CHUNK)
    y_s5 = _s5_scan(u.reshape(bsz, l, S5_WIDTH), ops, bsz, l)
    o_s5 = _s5_glu(y_s5, u, s5_d, cast(w_glu), b_glu)

    wb = cast(w_branch)
    merged = _branch_merge(o_attn, o_ssd, o_s5, wb[:Q_W], wb[Q_W:Q_W + SSD_INNER], wb[Q_W + SSD_INNER:], gates)
    return _matmul(merged, cast(w_out), name="out_proj")


def kernel(x, c, positions, w_mod, b_mod, norm_mix_pre, norm_mix_post, norm_ffn_pre, norm_ffn_post, w_in, attn_sinks, conv_w, conv_b, dt_bias, a_log, ssd_d, ssd_norm, s5_lam_re, s5_lam_im, s5_log_step, s5_b_re, s5_b_im, s5_c_re, s5_c_im, s5_d, s5_w_glu, s5_b_glu, w_branch, w_out, ffn_w1, ffn_w3, ffn_w2, w_router, moe_w1, moe_w3, moe_w2):
    bsz, l, d = x.shape
    depth = w_mod.shape[0]
    assert l % SSD_CHUNK == 0 and l % ATTN_BLOCK == 0 and l % S5_CHUNK == 0
    rope = _rope_tables(positions)
    mod = _modulation(c, w_mod, b_mod)
    cast = lambda w: w.astype(BF)

    def mods(i):
        return [mod[i, :, k * d:(k + 1) * d] for k in range(6)]

    sh1, sc1, g1, sh2, sc2, g2 = mods(0)
    h = _pre_norm(x, norm_mix_pre[0], sh1, sc1)
    for i in range(depth):
        m = _hybrid_mixer(h.reshape(bsz * l, d), rope, bsz, l, w_in[i], attn_sinks[i], conv_w[i], conv_b[i],
                          dt_bias[i], a_log[i], ssd_d[i], ssd_norm[i], s5_lam_re[i], s5_lam_im[i],
                          s5_log_step[i], s5_b_re[i], s5_b_im[i], s5_c_re[i], s5_c_im[i], s5_d[i],
                          s5_w_glu[i], s5_b_glu[i], w_branch[i], w_out[i])
        x, h = _post_norm(m, x, g1, norm_mix_post[i], norm_ffn_pre[i], sh2, sc2)
        h2 = h.reshape(bsz * l, d)
        j = i // 2
        if i % 2 == 0:
            act = _swiglu_up(h2, cast(ffn_w1[j]), cast(ffn_w3[j]))
            f = _matmul(act, cast(ffn_w2[j]), tm=512, name="ffn_down")
        else:
            f = _moe(h2, w_router[j], cast(moe_w1[j]), cast(moe_w3[j]), cast(moe_w2[j]))
        if i + 1 < depth:
            sh1, sc1, g1n, sh2n, sc2n, g2n = mods(i + 1)
            x, h = _post_norm(f, x, g2, norm_ffn_post[i], norm_mix_pre[i + 1], sh1, sc1)
            g1, sh2, sc2, g2 = g1n, sh2n, sc2n, g2n
        else:
            x, _ = _post_norm(f, x, g2, norm_ffn_post[i])
    return x
```

```python
import functools
import math

import jax
import jax.numpy as jnp
from jax import lax
from jax.experimental import pallas as pl
from jax.experimental.pallas import tpu as pltpu

BF = jnp.bfloat16
F32 = jnp.float32

HEAD_DIM = 64
Q_HEADS = 32
KV_HEADS = 4
Q_GROUP = Q_HEADS // KV_HEADS
Q_W = Q_HEADS * HEAD_DIM
KV_W = KV_HEADS * HEAD_DIM
ATTN_BLOCK = 128
ROT_DIM = 16
ROT_HALF = ROT_DIM // 2
ROPE_THETA = 500000.0
SSD_INNER = 2048
SSD_HEADS = 32
SSD_HEAD_DIM = 64
SSD_GROUPS = 4
SSD_STATE = 128
SSD_CONV = 4
SSD_CHUNK = 128
SSD_CONV_CH = SSD_INNER + 2 * SSD_GROUPS * SSD_STATE
S5_WIDTH = 1536
S5_GROUP_SIZE = 16
S5_GROUPS = 96
S5_STATE = 64
S5_CHUNK = 16
S5_GROUP_BLOCK = 8
N_EXPERTS = 8
RMS_EPS = 1e-6
LANES = 128
MIB = 1024 * 1024
NEG_BIG = -1e30


def _cparams(semantics, vmem_bytes):
    limit = int(min(max(vmem_bytes * 1.25 + 4 * MIB, 16 * MIB), 60 * MIB))
    return pltpu.CompilerParams(dimension_semantics=semantics, vmem_limit_bytes=limit)


def _pick(n, candidates):
    for c in candidates:
        if n % c == 0:
            return c
    raise ValueError(f"no tile in {candidates} divides {n}")


def _silu(x):
    return x * jax.nn.sigmoid(x)


def _mod_kernel(c_ref, w_ref, b_ref, o_ref):
    c = c_ref[...]
    a = _silu(c).astype(BF)
    o_ref[0] = jnp.dot(a, w_ref[0].astype(BF), preferred_element_type=F32) + b_ref[0]


def _modulation(c, w_mod, b_mod):
    depth, d, n = w_mod.shape
    bsz = c.shape[0]
    rows = 8
    c_pad = jnp.zeros((rows, d), F32).at[:bsz].set(c)
    tn = _pick(n, (512, 256, 128))
    out = pl.pallas_call(
        _mod_kernel,
        out_shape=jax.ShapeDtypeStruct((depth, rows, n), F32),
        grid=(depth, n // tn),
        in_specs=[pl.BlockSpec((rows, d), lambda l, j: (0, 0)),
                  pl.BlockSpec((1, d, tn), lambda l, j: (l, 0, j)),
                  pl.BlockSpec((1, 1, tn), lambda l, j: (l, 0, j))],
        out_specs=pl.BlockSpec((1, rows, tn), lambda l, j: (l, 0, j)),
        compiler_params=_cparams(("parallel", "arbitrary"), 2 * d * tn * 4 + d * tn * 2),
        name="adaln_modulation",
    )(c_pad, w_mod, b_mod.reshape(depth, 1, n))
    return out[:, :bsz]


def _rms(x, w):
    return x * lax.rsqrt(jnp.mean(x * x, axis=-1, keepdims=True) + RMS_EPS) * w


def _pre_norm_kernel(x_ref, w_ref, sh_ref, sc_ref, h_ref):
    h = _rms(x_ref[0], w_ref[...])
    h_ref[0] = (h * (1.0 + sc_ref[0]) + sh_ref[0]).astype(h_ref.dtype)


def _pre_norm(x, w, shift, scale):
    bsz, l, d = x.shape
    tl = _pick(l, (256, 128))
    vec = pl.BlockSpec((1, 1, d), lambda b, i: (b, 0, 0))
    return pl.pallas_call(
        _pre_norm_kernel,
        out_shape=jax.ShapeDtypeStruct((bsz, l, d), BF),
        grid=(bsz, l // tl),
        in_specs=[pl.BlockSpec((1, tl, d), lambda b, i: (b, i, 0)),
                  pl.BlockSpec((1, d), lambda b, i: (0, 0)), vec, vec],
        out_specs=pl.BlockSpec((1, tl, d), lambda b, i: (b, i, 0)),
        compiler_params=_cparams(("parallel", "parallel"), 2 * tl * d * 6 + 2 * tl * d * 4),
        name="pre_norm",
    )(x, w.reshape(1, d), shift.reshape(bsz, 1, d), scale.reshape(bsz, 1, d))


def _post_norm_kernel(m_ref, x_ref, g_ref, wpost_ref, wpre_ref, sh_ref, sc_ref, xo_ref, h_ref):
    xn = x_ref[0] + g_ref[0] * _rms(m_ref[0].astype(F32), wpost_ref[...])
    xo_ref[0] = xn
    h = _rms(xn, wpre_ref[...])
    h_ref[0] = (h * (1.0 + sc_ref[0]) + sh_ref[0]).astype(h_ref.dtype)


def _post_norm_last_kernel(m_ref, x_ref, g_ref, wpost_ref, xo_ref):
    xo_ref[0] = x_ref[0] + g_ref[0] * _rms(m_ref[0].astype(F32), wpost_ref[...])


def _post_norm(m, x, gate, w_post, w_pre=None, shift=None, scale=None, h_dtype=BF):
    bsz, l, d = x.shape
    tl = _pick(l, (256, 128))
    row = pl.BlockSpec((1, tl, d), lambda b, i: (b, i, 0))
    vec = pl.BlockSpec((1, 1, d), lambda b, i: (b, 0, 0))
    par = pl.BlockSpec((1, d), lambda b, i: (0, 0))
    m = m.reshape(bsz, l, d)
    gate = gate.reshape(bsz, 1, d)
    cp = _cparams(("parallel", "parallel"), 2 * tl * d * (2 + 4 + 4 + 2) + 3 * tl * d * 4)
    if w_pre is None:
        return pl.pallas_call(
            _post_norm_last_kernel,
            out_shape=jax.ShapeDtypeStruct((bsz, l, d), F32),
            grid=(bsz, l // tl),
            in_specs=[row, row, vec, par],
            out_specs=row,
            compiler_params=cp,
            name="post_norm_last",
        )(m, x, gate, w_post.reshape(1, d)), None
    return pl.pallas_call(
        _post_norm_kernel,
        out_shape=(jax.ShapeDtypeStruct((bsz, l, d), F32), jax.ShapeDtypeStruct((bsz, l, d), h_dtype)),
        grid=(bsz, l // tl),
        in_specs=[row, row, vec, par, par, vec, vec],
        out_specs=(row, row),
        compiler_params=cp,
        name="post_norm",
    )(m, x, gate, w_post.reshape(1, d), w_pre.reshape(1, d), shift.reshape(bsz, 1, d),
      scale.reshape(bsz, 1, d))


def _mm_call(body, a_list, w_list, extras, extra_specs, *, n, tm, tn, out_dtype, name):
    m = a_list[0].shape[0]
    tm = min(tm, m)
    assert m % tm == 0 and n % tn == 0
    a_specs = [pl.BlockSpec((tm, a.shape[1]), lambda i, j: (i, 0)) for a in a_list]
    w_specs = [pl.BlockSpec((w.shape[0], tn), lambda i, j: (0, j)) for w in w_list]
    vmem = sum(2 * tm * a.shape[1] * a.dtype.itemsize for a in a_list)
    vmem += sum(2 * w.shape[0] * tn * w.dtype.itemsize for w in w_list)
    vmem += 2 * tm * tn * 4 * (1 + len(extras)) + 3 * tm * tn * 4
    return pl.pallas_call(
        body,
        out_shape=jax.ShapeDtypeStruct((m, n), out_dtype),
        grid=(m // tm, n // tn),
        in_specs=a_specs + w_specs + list(extra_specs(tm, tn)),
        out_specs=pl.BlockSpec((tm, tn), lambda i, j: (i, j)),
        compiler_params=_cparams(("parallel", "arbitrary"), vmem),
        name=name,
    )(*a_list, *w_list, *extras)


def _dot(a, w):
    return jnp.dot(a, w, preferred_element_type=F32)


def _plain_body(a_ref, w_ref, o_ref):
    o_ref[...] = _dot(a_ref[...], w_ref[...]).astype(o_ref.dtype)


def _matmul(a, w, *, tn=None, tm=1024, out_dtype=BF, body=_plain_body, name="matmul"):
    n = w.shape[1]
    tn = tn or _pick(n, (512, 256, 128))
    return _mm_call(body, [a], [w], [], lambda tm_, tn_: [], n=n, tm=tm, tn=tn,
                    out_dtype=out_dtype, name=name)


def _wres_kernel(n_w, epilogue, a_ref, *refs):
    w_refs, extras, o_ref, wb = refs[:n_w], refs[n_w:-n_w - 1], refs[-n_w - 1], refs[-n_w:]

    @pl.when(pl.program_id(1) == 0)
    def _():
        for w_ref, b in zip(w_refs, wb):
            b[...] = w_ref[0].astype(BF)

    a = a_ref[...]
    epilogue([_dot(a, b[...]) for b in wb], extras, o_ref)


def _mm_wres(a, weights, epilogue, *, n, col0=0, tn=None, extras=(), extra_specs=(), tm=1024,
             out_dtype=BF, name):
    m, k = a.shape
    tm = min(tm, m)
    tn = tn or next(t for t in (512, 256, 128) if n % t == 0 and col0 % t == 0)
    assert m % tm == 0 and n % tn == 0 and col0 % tn == 0
    c0 = col0 // tn
    w_specs = [pl.BlockSpec((1, k, tn), functools.partial(lambda lead, j, i: (lead, 0, c0 + j), lead))
               for _, lead in weights]
    vmem = 2 * tm * k * 2 + len(weights) * (2 * k * tn * 4 + k * tn * 2) + 2 * tm * tn * 4 + 4 * tm * tn * 4
    return pl.pallas_call(
        functools.partial(_wres_kernel, len(weights), epilogue),
        out_shape=jax.ShapeDtypeStruct((m, n), out_dtype),
        grid=(n // tn, m // tm),
        in_specs=[pl.BlockSpec((tm, k), lambda j, i: (i, 0))] + w_specs + list(extra_specs),
        out_specs=pl.BlockSpec((tm, tn), lambda j, i: (i, j)),
        scratch_shapes=[pltpu.VMEM((k, tn), BF)] * len(weights),
        compiler_params=_cparams(("arbitrary", "arbitrary"), vmem),
        name=name,
    )(a, *[w for w, _ in weights], *extras)


def _ep_plain(accs, extras, o_ref):
    o_ref[...] = accs[0].astype(o_ref.dtype)


def _ep_sigmoid(accs, extras, o_ref):
    o_ref[...] = jax.nn.sigmoid(accs[0]).astype(o_ref.dtype)


def _ep_swiglu(accs, extras, o_ref):
    o_ref[...] = (_silu(accs[0]) * accs[1]).astype(o_ref.dtype)


def _ep_qkv(accs, extras, o_ref):
    cos_ref, sa_ref, sb_ref = extras
    j = pl.program_id(0)
    acc = accs[0]
    tn = acc.shape[1]
    n_tiles = (Q_W + 2 * KV_W) // tn
    n_q_tiles = Q_W // tn

    def emit(tile, scale):
        cols = []
        for c in range(tn // LANES):
            r = acc[:, c * LANES:(c + 1) * LANES]
            if tile * tn + c * LANES < Q_W + KV_W:
                r = (r * cos_ref[...] + pltpu.roll(r, ROT_HALF, axis=1) * sa_ref[...]
                     + pltpu.roll(r, LANES - ROT_HALF, axis=1) * sb_ref[...])
            cols.append(r)
        out = jnp.concatenate(cols, axis=1)
        o_ref[...] = (out * scale if scale != 1.0 else out).astype(o_ref.dtype)

    @pl.when(j < n_q_tiles)
    def _():
        emit(0, HEAD_DIM ** -0.5)

    for tile in range(n_q_tiles, n_tiles):
        @pl.when(j == tile)
        def _(tile=tile):
            emit(tile, 1.0)


def _merge_body(oa_ref, ob_ref, oc_ref, wa_ref, wb_ref, wc_ref, ga_ref, gb_ref, gc_ref, o_ref):
    acc = ga_ref[...].astype(F32) * _dot(oa_ref[...], wa_ref[...])
    acc += gb_ref[...].astype(F32) * _dot(ob_ref[...], wb_ref[...])
    acc += gc_ref[...].astype(F32) * _dot(oc_ref[...], wc_ref[...])
    o_ref[...] = acc.astype(o_ref.dtype)


def _branch_merge(o_attn, o_ssd, o_s5, wa, wb, wc, gates):
    d = wa.shape[1]
    tn = _pick(d, (512, 256, 128))
    nt = d // tn

    def gate_specs(tm_, tn_):
        return [pl.BlockSpec((tm_, tn_), functools.partial(lambda br, i, j: (i, br * nt + j), br))
                for br in range(3)]

    return _mm_call(_merge_body, [o_attn, o_ssd, o_s5], [wa, wb, wc], [gates, gates, gates],
                    gate_specs, n=d, tm=512, tn=tn, out_dtype=BF, name="branch_merge")


def _attn_kernel(sink_ref, q_ref, kc_ref, kp_ref, vc_ref, vp_ref, o_ref):
    n = pl.program_id(1)
    qi = lax.broadcasted_iota(jnp.int32, (ATTN_BLOCK, 2 * ATTN_BLOCK), 0)
    sj = lax.broadcasted_iota(jnp.int32, (ATTN_BLOCK, 2 * ATTN_BLOCK), 1)
    rel = qi + ATTN_BLOCK - sj
    valid = (rel >= 0) & (rel < ATTN_BLOCK) & ((sj >= ATTN_BLOCK) | (n > 0))
    q = q_ref[0]
    kc, kp, vc, vp = kc_ref[0], kp_ref[0], vc_ref[0], vp_ref[0]
    outs = []
    for g in range(KV_HEADS):
        sl = slice(g * HEAD_DIM, (g + 1) * HEAD_DIM)
        kb = jnp.concatenate([kp[:, sl], kc[:, sl]], axis=0)
        vb = jnp.concatenate([vp[:, sl], vc[:, sl]], axis=0)
        for r in range(Q_GROUP):
            hd = g * Q_GROUP + r
            qh = q[:, hd * HEAD_DIM:(hd + 1) * HEAD_DIM]
            s = lax.dot_general(qh, kb, (((1,), (1,)), ((), ())), preferred_element_type=F32)
            s = jnp.where(valid, s, NEG_BIG)
            sink = sink_ref[hd]
            mx = jnp.maximum(jnp.max(s, axis=-1, keepdims=True), sink)
            p = jnp.exp(s - mx)
            den = jnp.sum(p, axis=-1, keepdims=True) + jnp.exp(sink - mx)
            o = _dot(p.astype(BF), vb)
            outs.append(o / den)
    o_ref[0] = jnp.concatenate(outs, axis=1).astype(o_ref.dtype)


def _attention(qkv, sinks, bsz, l):
    nb = l // ATTN_BLOCK
    qkv = qkv.reshape(bsz, l, Q_W + 2 * KV_W)
    kcol, vcol = Q_W // KV_W, Q_W // KV_W + 1
    cur = lambda col: pl.BlockSpec((1, ATTN_BLOCK, KV_W), lambda b, n: (b, n, col))
    prev = lambda col: pl.BlockSpec((1, ATTN_BLOCK, KV_W), lambda b, n: (b, jnp.maximum(n - 1, 0), col))
    out = pl.pallas_call(
        _attn_kernel,
        out_shape=jax.ShapeDtypeStruct((bsz, l, Q_W), BF),
        grid=(bsz, nb),
        in_specs=[pl.BlockSpec(memory_space=pltpu.SMEM),
                  pl.BlockSpec((1, ATTN_BLOCK, Q_W), lambda b, n: (b, n, 0)),
                  cur(kcol), prev(kcol), cur(vcol), prev(vcol)],
        out_specs=pl.BlockSpec((1, ATTN_BLOCK, Q_W), lambda b, n: (b, n, 0)),
        compiler_params=_cparams(("parallel", "parallel"), 8 * MIB),
        name="sliding_window_attention",
    )(sinks.astype(F32), qkv, qkv, qkv, qkv, qkv)
    return out.reshape(bsz * l, Q_W)


def _split_dot(f, e):
    hi = f.astype(BF)
    lo = (f - hi.astype(F32)).astype(BF)
    return _dot(hi, e) + _dot(lo, e)


def _ssd_kernel(z_ref, xbc_ref, dt_ref, cw_ref, cb_ref, dtb_ref, alog_ref, dsk_ref, nw_ref,
                o_ref, ext_ref, state_ref):
    c = pl.program_id(1)
    q = SSD_CHUNK
    halo = 8

    @pl.when(c == 0)
    def _():
        ext_ref[0:halo, :] = jnp.zeros((halo, SSD_CONV_CH), F32)
        state_ref[...] = jnp.zeros_like(state_ref)

    ext_ref[halo:halo + q, :] = xbc_ref[0].astype(F32)
    conv = cb_ref[...] + cw_ref[SSD_CONV - 1:SSD_CONV, :] * ext_ref[halo:halo + q, :]
    for j in range(SSD_CONV - 1):
        off = halo - (SSD_CONV - 1) + j
        conv = conv + cw_ref[j:j + 1, :] * ext_ref[off:off + q, :]
    tail = ext_ref[q:q + halo, :]
    ext_ref[0:halo, :] = tail
    xbc = _silu(conv)
    xs = xbc[:, :SSD_INNER]
    xs_bf = xs.astype(BF)
    bm = xbc[:, SSD_INNER:SSD_INNER + SSD_GROUPS * SSD_STATE].astype(BF)
    cm = xbc[:, SSD_INNER + SSD_GROUPS * SSD_STATE:].astype(BF)

    dt = jax.nn.softplus(dt_ref[0] + dtb_ref[...])
    a = -jnp.exp(alog_ref[...])
    cs = dt * a
    row = lax.broadcasted_iota(jnp.int32, (q, LANES), 0)
    sh = 1
    while sh < q:
        cs = cs + jnp.where(row >= sh, pltpu.roll(cs, sh, axis=0), 0.0)
        sh *= 2
    cs_last = cs[q - 1:q, :]
    ecs = jnp.exp(cs)
    dte = jnp.exp(cs_last - cs) * dt
    cs_t = cs.T
    dt_t = dt.T

    hl = lax.broadcasted_iota(jnp.int32, (LANES, SSD_INNER), 0)
    hc = lax.broadcasted_iota(jnp.int32, (LANES, SSD_INNER), 1)
    expand = jnp.where(hl == hc // SSD_HEAD_DIM, 1.0, 0.0).astype(BF)
    ecs_x = _split_dot(ecs, expand)
    dte_x = _split_dot(dte, expand)

    li = lax.broadcasted_iota(jnp.int32, (q, q), 0)
    si = lax.broadcasted_iota(jnp.int32, (q, q), 1)
    causal = li >= si
    heads_per_group = SSD_HEADS // SSD_GROUPS
    gw = heads_per_group * SSD_HEAD_DIM
    y_parts = []
    for g in range(SSD_GROUPS):
        ns = slice(g * SSD_STATE, (g + 1) * SSD_STATE)
        gs = slice(g * gw, (g + 1) * gw)
        bm_g, cm_g = bm[:, ns], cm[:, ns]
        cb = lax.dot_general(cm_g, bm_g, (((1,), (1,)), ((), ())), preferred_element_type=F32)
        for r in range(heads_per_group):
            hd = g * heads_per_group + r
            seg = cs[:, hd:hd + 1] - cs_t[hd:hd + 1, :]
            decay = jnp.exp(jnp.where(causal, seg, NEG_BIG))
            w = (cb * decay * dt_t[hd:hd + 1, :]).astype(BF)
            y_parts.append(_dot(w, xs_bf[:, hd * SSD_HEAD_DIM:(hd + 1) * SSD_HEAD_DIM]))
    y = jnp.concatenate(y_parts, axis=1)
    xw = (xs * dte_x).astype(BF)
    for g in range(SSD_GROUPS):
        ns = slice(g * SSD_STATE, (g + 1) * SSD_STATE)
        gs = slice(g * gw, (g + 1) * gw)
        prev = state_ref[:, gs]
        y_off = _dot(cm[:, ns], prev.astype(BF)) * ecs_x[:, gs]
        upd = lax.dot_general(bm[:, ns], xw[:, gs], (((0,), (0,)), ((), ())),
                              preferred_element_type=F32)
        state_ref[:, gs] = prev * ecs_x[q - 1:q, gs] + upd
        yg = y[:, gs] + y_off + dsk_ref[:, gs] * xs[:, gs]
        yg = yg * _silu(z_ref[0, :, gs].astype(F32))
        yg = yg * lax.rsqrt(jnp.mean(yg * yg, axis=-1, keepdims=True) + RMS_EPS)
        o_ref[0, :, gs] = (yg * nw_ref[:, gs]).astype(o_ref.dtype)


def _ssd(z, xbc, dt_raw, conv_w, conv_b, dt_bias, a_log, d_skip, norm_w, bsz, l):
    q = SSD_CHUNK
    pad = LANES - SSD_HEADS
    row = lambda w: pl.BlockSpec((1, q, w), lambda b, c: (b, c, 0))
    par = lambda r, w: pl.BlockSpec((r, w), lambda b, c: (0, 0))
    out = pl.pallas_call(
        _ssd_kernel,
        out_shape=jax.ShapeDtypeStruct((bsz, l, SSD_INNER), BF),
        grid=(bsz, l // q),
        in_specs=[row(SSD_INNER), row(SSD_CONV_CH), row(LANES),
                  par(SSD_CONV, SSD_CONV_CH), par(1, SSD_CONV_CH), par(1, LANES), par(1, LANES),
                  par(1, SSD_INNER), par(1, SSD_INNER)],
        out_specs=row(SSD_INNER),
        scratch_shapes=[pltpu.VMEM((q + 8, SSD_CONV_CH), F32), pltpu.VMEM((SSD_STATE, SSD_INNER), F32)],
        compiler_params=_cparams(("parallel", "arbitrary"), 24 * MIB),
        name="ssd_chunked",
    )(z.reshape(bsz, l, SSD_INNER), xbc.reshape(bsz, l, SSD_CONV_CH), dt_raw.reshape(bsz, l, LANES),
      conv_w, conv_b.reshape(1, -1), jnp.pad(dt_bias, (0, pad)).reshape(1, LANES),
      jnp.pad(a_log, (0, pad)).reshape(1, LANES),
      jnp.repeat(d_skip, SSD_HEAD_DIM).reshape(1, SSD_INNER), norm_w.reshape(1, SSD_INNER))
    return out.reshape(bsz * l, SSD_INNER)


def _s5_operators(lam_re, lam_im, log_step, b_re, b_im, c_re, c_im, n_chunks):
    hp = lax.Precision.HIGHEST
    qn = S5_CHUNK
    lr, li = lam_re.astype(F32), lam_im.astype(F32)
    step = jnp.exp(log_step.astype(F32))[:, None]
    mag = jnp.exp(lr * step)
    a_re, a_im = mag * jnp.cos(li * step), mag * jnp.sin(li * step)
    den = lr * lr + li * li
    coef_re = ((a_re - 1.0) * lr + a_im * li) / den
    coef_im = (a_im * lr - (a_re - 1.0) * li) / den
    br, bi = b_re.astype(F32), b_im.astype(F32)
    bb_re = coef_re[..., None] * br - coef_im[..., None] * bi
    bb_im = coef_re[..., None] * bi + coef_im[..., None] * br

    def power(d):
        d = d.astype(F32)[None, :, None]
        m = jnp.exp(lr[:, None, :] * step[:, None, :] * d)
        ang = li[:, None, :] * step[:, None, :] * d
        return m * jnp.cos(ang), m * jnp.sin(ang)

    p_re, p_im = power(jnp.arange(qn + 1))
    cr, ci = c_re.astype(F32), c_im.astype(F32)
    m_re = cr[:, None] * p_re[:, :, None, :] - ci[:, None] * p_im[:, :, None, :]
    m_im = cr[:, None] * p_im[:, :, None, :] + ci[:, None] * p_re[:, :, None, :]
    kern = (jnp.einsum('gdip,gpj->gdij', m_re, bb_re, precision=hp)
            - jnp.einsum('gdip,gpj->gdij', m_im, bb_im, precision=hp))
    s_idx = jnp.arange(qn)[:, None]
    t_idx = jnp.arange(qn)[None, :]
    lag = t_idx - s_idx
    toep = kern[:, jnp.clip(lag, 0, qn)]
    toep = jnp.where((lag >= 0)[None, :, :, None, None], toep, 0.0)
    toep = toep.transpose(0, 1, 4, 2, 3).reshape(S5_GROUPS, qn * S5_GROUP_SIZE, qn * S5_GROUP_SIZE)
    e_re, e_im = p_re[:, qn - 1 - jnp.arange(qn)], p_im[:, qn - 1 - jnp.arange(qn)]
    be_re = e_re[:, :, None, :] * bb_re.transpose(0, 2, 1)[:, None] - e_im[:, :, None, :] * bb_im.transpose(0, 2, 1)[:, None]
    be_im = e_re[:, :, None, :] * bb_im.transpose(0, 2, 1)[:, None] + e_im[:, :, None, :] * bb_re.transpose(0, 2, 1)[:, None]
    b_end = jnp.concatenate([be_re, be_im], axis=-1).reshape(S5_GROUPS, qn * S5_GROUP_SIZE, 2 * S5_STATE)
    ci_re = m_re[:, 1:].transpose(0, 3, 1, 2)
    ci_im = -m_im[:, 1:].transpose(0, 3, 1, 2)
    c_in = jnp.concatenate([ci_re, ci_im], axis=1).reshape(S5_GROUPS, 2 * S5_STATE, qn * S5_GROUP_SIZE)
    n_steps = max(1, (n_chunks - 1).bit_length())
    s_re, s_im = power(qn * (2 ** jnp.arange(n_steps)))
    pw1 = jnp.concatenate([s_re, s_re], axis=-1)
    pw2 = jnp.concatenate([-s_im, s_im], axis=-1)
    return toep.astype(BF), b_end.astype(BF), c_in.astype(BF), pw1, pw2


def _s5_kernel(u_ref, toep_ref, bend_ref, cin_ref, pw1_ref, pw2_ref, y_ref):
    n_chunks = u_ref.shape[2]
    n_steps = pw1_ref.shape[1]
    row = lax.broadcasted_iota(jnp.int32, (n_chunks, 2 * S5_STATE), 0)
    for g in range(u_ref.shape[0]):
        u = u_ref[g, 0]
        x = _dot(u, bend_ref[g])
        for k in range(n_steps):
            sh = 1 << k
            xs = jnp.where(row >= sh, pltpu.roll(x, sh, axis=0), 0.0)
            x = x + pw1_ref[g, k:k + 1, :] * xs + pw2_ref[g, k:k + 1, :] * pltpu.roll(xs, S5_STATE, axis=1)
        x_in = jnp.where(row >= 1, pltpu.roll(x, 1, axis=0), 0.0)
        y = _dot(u, toep_ref[g]) + _dot(x_in.astype(BF), cin_ref[g])
        y_ref[g, 0] = y.astype(y_ref.dtype)


def _s5_scan(u, ops, bsz, l):
    toep, b_end, c_in, pw1, pw2 = ops
    qn, gs, ng, gb = S5_CHUNK, S5_GROUP_SIZE, S5_GROUPS, S5_GROUP_BLOCK
    nc = l // qn
    w = qn * gs
    ug = u.reshape(bsz, nc, qn, ng, gs).transpose(3, 0, 1, 2, 4).reshape(ng, bsz, nc, w)
    wspec = lambda r, c: pl.BlockSpec((gb, r, c), lambda g, b: (g, 0, 0))
    y = pl.pallas_call(
        _s5_kernel,
        out_shape=jax.ShapeDtypeStruct((ng, bsz, nc, w), BF),
        grid=(ng // gb, bsz),
        in_specs=[pl.BlockSpec((gb, 1, nc, w), lambda g, b: (g, b, 0, 0)),
                  wspec(w, w), wspec(w, 2 * S5_STATE), wspec(2 * S5_STATE, w),
                  wspec(pw1.shape[1], 2 * S5_STATE), wspec(pw1.shape[1], 2 * S5_STATE)],
        out_specs=pl.BlockSpec((gb, 1, nc, w), lambda g, b: (g, b, 0, 0)),
        compiler_params=_cparams(("parallel", "arbitrary"), 16 * MIB),
        name="s5_chunked_scan",
    )(ug, toep, b_end, c_in, pw1, pw2)
    return y.reshape(ng, bsz, nc, qn, gs).transpose(1, 2, 3, 0, 4).reshape(bsz * l, S5_WIDTH)


def _s5_glu_kernel(y_ref, u_ref, d_ref, w_ref, b_ref, o_ref):
    y = y_ref[...].astype(F32) + d_ref[...] * u_ref[...].astype(F32)
    v = jax.nn.gelu(y, approximate=True)
    o_ref[...] = (v * jax.nn.sigmoid(_dot(v.astype(BF), w_ref[...]) + b_ref[...])).astype(o_ref.dtype)


def _s5_glu(y, u, d_skip, w_glu, b_glu):
    m, w = y.shape
    tm = _pick(m, (512, 256, 128))
    row = pl.BlockSpec((tm, w), lambda i: (i, 0))
    par = pl.BlockSpec((1, w), lambda i: (0, 0))
    return pl.pallas_call(
        _s5_glu_kernel,
        out_shape=jax.ShapeDtypeStruct((m, w), BF),
        grid=(m // tm,),
        in_specs=[row, row, par, pl.BlockSpec((w, w), lambda i: (0, 0)), par],
        out_specs=row,
        compiler_params=_cparams(("parallel",), 6 * tm * w * 2 + 2 * w * w * 2 + 4 * tm * w * 4),
        name="s5_gelu_glu",
    )(y, u, d_skip.reshape(1, w), w_glu, b_glu.reshape(1, w))


MOE_ROW_TILE = 256
INFO_E1, INFO_E2, INFO_W1, INFO_W2, INFO_R1, INFO_R2 = range(6)


def _split_bf16(x):
    hi = x.astype(BF)
    return hi, (x - hi.astype(F32)).astype(BF)


def _router_kernel(h_ref, w_ref, info_ref, cnt_ref, carry_ref):
    i = pl.program_id(0)

    @pl.when(i == 0)
    def _():
        carry_ref[...] = jnp.zeros_like(carry_ref)

    hh, hl = _split_bf16(h_ref[...])
    wh, wl = _split_bf16(w_ref[...])
    logits = _dot(hh, wh) + _dot(hl, wh) + _dot(hh, wl)
    tm = logits.shape[0]
    lane = lax.broadcasted_iota(jnp.int32, logits.shape, 1)
    lg = jnp.where(lane < N_EXPERTS, logits, -jnp.inf)
    m1 = jnp.max(lg, axis=-1, keepdims=True)
    i1 = jnp.min(jnp.where(lg == m1, lane, LANES), axis=-1, keepdims=True)
    lg2 = jnp.where(lane == i1, -jnp.inf, lg)
    m2 = jnp.max(lg2, axis=-1, keepdims=True)
    i2 = jnp.min(jnp.where(lg2 == m2, lane, LANES), axis=-1, keepdims=True)
    e = jnp.exp(m2 - m1)
    w1 = 1.0 / (1.0 + e)
    w2 = e * w1
    onehot = jnp.where(lane == i1, 1.0, 0.0) + jnp.where(lane == i2, 1.0, 0.0)
    rt = lax.broadcasted_iota(jnp.int32, (tm, tm), 0)
    ct = lax.broadcasted_iota(jnp.int32, (tm, tm), 1)
    before = jnp.where(rt > ct, 1.0, 0.0).astype(BF)
    rank = _dot(before, onehot.astype(BF)) + carry_ref[...]
    r1 = jnp.sum(jnp.where(lane == i1, rank, 0.0), axis=-1, keepdims=True)
    r2 = jnp.sum(jnp.where(lane == i2, rank, 0.0), axis=-1, keepdims=True)
    carry_ref[...] += jnp.sum(onehot, axis=0, keepdims=True)
    cnt_ref[...] = carry_ref[...]
    info = jnp.zeros_like(logits)
    for slot, val in ((INFO_E1, i1.astype(F32)), (INFO_E2, i2.astype(F32)), (INFO_W1, w1), (INFO_W2, w2),
                      (INFO_R1, r1), (INFO_R2, r2)):
        info = jnp.where(lane == slot, val, info)
    info_ref[...] = info


def _router(h, w_router):
    m, d = h.shape
    tm = _pick(m, (512, 256, 128))
    w_pad = jnp.zeros((d, LANES), F32).at[:, :N_EXPERTS].set(w_router.astype(F32))
    return pl.pallas_call(
        _router_kernel,
        out_shape=(jax.ShapeDtypeStruct((m, LANES), F32), jax.ShapeDtypeStruct((1, LANES), F32)),
        grid=(m // tm,),
        in_specs=[pl.BlockSpec((tm, d), lambda i: (i, 0)), pl.BlockSpec((d, LANES), lambda i: (0, 0))],
        out_specs=(pl.BlockSpec((tm, LANES), lambda i: (i, 0)), pl.BlockSpec((1, LANES), lambda i: (0, 0))),
        scratch_shapes=[pltpu.VMEM((1, LANES), F32)],
        compiler_params=_cparams(("arbitrary",), 2 * tm * d * 4 + 2 * tm * d * 2 + 2 * d * LANES * 4
                                 + 2 * tm * tm * 2 + 8 * tm * LANES * 4),
        name="moe_router",
    )(h, w_pad)


def _row_copy(src_hbm, src_row, buf, slot, dst_row, sem):
    return pltpu.make_async_copy(src_hbm.at[pl.ds(src_row, 1)], buf.at[slot, pl.ds(dst_row, 1)], sem.at[slot])


def _dispatch_kernel(tok_ref, nu_ref, h_hbm, o_ref, buf, sem):
    i = pl.program_id(0)
    rows = o_ref.shape[0]
    nu = nu_ref[0]

    def issue(tile, slot):
        def body(r, carry):
            _row_copy(h_hbm, tok_ref[tile * rows + r], buf, slot, r, sem).start()
            return carry
        lax.fori_loop(0, rows, body, 0, unroll=8)

    def wait(slot):
        def body(r, carry):
            _row_copy(h_hbm, 0, buf, slot, 0, sem).wait()
            return carry
        lax.fori_loop(0, rows, body, 0, unroll=8)

    @pl.when(i == 0)
    def _():
        issue(0, 0)

    @pl.when(i + 1 < nu)
    def _():
        issue(i + 1, (i + 1) % 2)

    @pl.when(i < nu)
    def _():
        wait(i % 2)
        o_ref[...] = buf[i % 2].astype(o_ref.dtype)

    @pl.when(i >= nu)
    def _():
        o_ref[...] = jnp.zeros_like(o_ref)


def _dispatch(h, tok_of_row, n_used, n_rows):
    d = h.shape[1]
    rt = MOE_ROW_TILE
    return pl.pallas_call(
        _dispatch_kernel,
        out_shape=jax.ShapeDtypeStruct((n_rows, d), BF),
        grid_spec=pltpu.PrefetchScalarGridSpec(
            num_scalar_prefetch=2,
            grid=(n_rows // rt,),
            in_specs=[pl.BlockSpec(memory_space=pl.ANY)],
            out_specs=pl.BlockSpec((rt, d), lambda i, tok, nu: (i, 0)),
            scratch_shapes=[pltpu.VMEM((2, rt, d), F32), pltpu.SemaphoreType.DMA((2,))]),
        compiler_params=_cparams(("arbitrary",), 2 * rt * d * 4 + 2 * rt * d * 2 + rt * d * 4),
        name="moe_dispatch",
    )(tok_of_row, n_used, h)


def _combine_kernel(p1_ref, p2_ref, y_hbm, o_ref, buf, sem):
    i = pl.program_id(0)
    n = pl.num_programs(0)
    rows = o_ref.shape[0]

    def issue(tile, slot):
        def body(r, carry):
            t = tile * rows + r
            _row_copy(y_hbm, p1_ref[t], buf, slot, r, sem).start()
            _row_copy(y_hbm, p2_ref[t], buf, slot, rows + r, sem).start()
            return carry
        lax.fori_loop(0, rows, body, 0, unroll=4)

    def wait(slot):
        def body(r, carry):
            _row_copy(y_hbm, 0, buf, slot, 0, sem).wait()
            return carry
        lax.fori_loop(0, 2 * rows, body, 0, unroll=8)

    @pl.when(i == 0)
    def _():
        issue(0, 0)

    @pl.when(i + 1 < n)
    def _():
        issue(i + 1, (i + 1) % 2)

    wait(i % 2)
    both = buf[i % 2]
    o_ref[...] = (both[:rows] + both[rows:]).astype(o_ref.dtype)


def _combine(y_rows, pos1, pos2, n_tokens):
    d = y_rows.shape[1]
    tt = 128
    return pl.pallas_call(
        _combine_kernel,
        out_shape=jax.ShapeDtypeStruct((n_tokens, d), BF),
        grid_spec=pltpu.PrefetchScalarGridSpec(
            num_scalar_prefetch=2,
            grid=(n_tokens // tt,),
            in_specs=[pl.BlockSpec(memory_space=pl.ANY)],
            out_specs=pl.BlockSpec((tt, d), lambda i, p1, p2: (i, 0)),
            scratch_shapes=[pltpu.VMEM((2, 2 * tt, d), F32), pltpu.SemaphoreType.DMA((2,))]),
        compiler_params=_cparams(("arbitrary",), 4 * tt * d * 4 + 2 * tt * d * 2 + 2 * tt * d * 4),
        name="moe_combine",
    )(pos1, pos2, y_rows)


def _expert_changed(te_ref, i):
    return jnp.logical_or(i == 0, te_ref[i] != te_ref[jnp.maximum(i - 1, 0)])


def _moe_up_kernel(te_ref, nu_ref, a_ref, w1_ref, w3_ref, rw_ref, o_ref, w1b, w3b):
    i = pl.program_id(1)
    valid = i < nu_ref[0]

    @pl.when(jnp.logical_and(valid, _expert_changed(te_ref, i)))
    def _():
        w1b[...] = w1_ref[0].astype(BF)
        w3b[...] = w3_ref[0].astype(BF)

    @pl.when(valid)
    def _():
        a = a_ref[...]
        act = _silu(_dot(a, w1b[...])) * _dot(a, w3b[...])
        o_ref[...] = (act * rw_ref[...]).astype(o_ref.dtype)

    @pl.when(jnp.logical_not(valid))
    def _():
        o_ref[...] = jnp.zeros_like(o_ref)


def _moe_down_kernel(te_ref, nu_ref, a_ref, w_ref, o_ref, wb):
    i = pl.program_id(1)
    valid = i < nu_ref[0]

    @pl.when(jnp.logical_and(valid, _expert_changed(te_ref, i)))
    def _():
        wb[...] = w_ref[0].astype(BF)

    @pl.when(valid)
    def _():
        o_ref[...] = _dot(a_ref[...], wb[...]).astype(o_ref.dtype)

    @pl.when(jnp.logical_not(valid))
    def _():
        o_ref[...] = jnp.zeros_like(o_ref)


def _grouped_call(kern, rows_in, weights, extras, *, tn, out_dtype, tile_expert, n_used, name):
    n_rows, k = rows_in.shape
    n = weights[0].shape[2]
    rt = MOE_ROW_TILE
    last = lambda i, nu: jnp.minimum(i, nu[0] - 1)
    row_spec = lambda w: pl.BlockSpec((rt, w), lambda j, i, te, nu: (last(i, nu), 0))
    w_spec = pl.BlockSpec((1, k, tn), lambda j, i, te, nu: (te[last(i, nu)], 0, j))
    vmem = 2 * rt * k * 2 + len(weights) * (2 * k * tn * 4 + k * tn * 2) + 2 * rt * tn * 4 + 4 * rt * tn * 4
    return pl.pallas_call(
        kern,
        out_shape=jax.ShapeDtypeStruct((n_rows, n), out_dtype),
        grid_spec=pltpu.PrefetchScalarGridSpec(
            num_scalar_prefetch=2,
            grid=(n // tn, n_rows // rt),
            in_specs=[row_spec(k)] + [w_spec] * len(weights) + [row_spec(e.shape[1]) for e in extras],
            out_specs=pl.BlockSpec((rt, tn), lambda j, i, te, nu: (i, j)),
            scratch_shapes=[pltpu.VMEM((k, tn), BF)] * len(weights)),
        compiler_params=_cparams(("arbitrary", "arbitrary"), vmem),
        name=name,
    )(tile_expert, n_used, rows_in, *weights, *extras)


def _moe(h, w_router, w1, w3, w2):
    m, d = h.shape
    ne, _, f = w1.shape
    rt = MOE_ROW_TILE
    n_rows = 2 * m + ne * rt
    info, counts = _router(h, w_router)
    e1, e2 = info[:, INFO_E1].astype(jnp.int32), info[:, INFO_E2].astype(jnp.int32)
    r1, r2 = info[:, INFO_R1].astype(jnp.int32), info[:, INFO_R2].astype(jnp.int32)
    tiles = (counts[0, :ne].astype(jnp.int32) + rt - 1) // rt
    cum_tiles = jnp.cumsum(tiles)
    base = (cum_tiles - tiles) * rt
    pos1, pos2 = base[e1] + r1, base[e2] + r2
    tok = jnp.arange(m, dtype=jnp.int32)
    tok_of_row = jnp.zeros((n_rows,), jnp.int32).at[pos1].set(tok).at[pos2].set(tok)
    row_w = jnp.zeros((n_rows,), F32).at[pos1].set(info[:, INFO_W1]).at[pos2].set(info[:, INFO_W2])
    n_used = cum_tiles[-1:].astype(jnp.int32)
    tile_expert = jnp.minimum(jnp.searchsorted(cum_tiles, jnp.arange(n_rows // rt, dtype=jnp.int32),
                                               side='right'), ne - 1).astype(jnp.int32)
    rows = _dispatch(h, tok_of_row, n_used, n_rows)
    grouped = functools.partial(_grouped_call, tile_expert=tile_expert, n_used=n_used)
    act = grouped(_moe_up_kernel, rows, [w1, w3], [row_w.reshape(n_rows, 1)],
                  tn=_pick(f, (512, 256, 128)), out_dtype=BF, name="moe_up")
    y_rows = grouped(_moe_down_kernel, act, [w2], [], tn=_pick(d, (1024, 512, 256, 128)), out_dtype=F32,
                     name="moe_down")
    return _combine(y_rows, pos1, pos2, m)


def _rope_tables(positions):
    inv_freq = ROPE_THETA ** (-jnp.arange(0, ROT_DIM, 2, dtype=F32) / ROT_DIM)
    ang = positions.astype(F32)[..., None] * inv_freq
    cos, sin = jnp.cos(ang), jnp.sin(ang)
    rest = HEAD_DIM - ROT_DIM
    ones = jnp.ones(ang.shape[:-1] + (rest,), F32)
    zeros = jnp.zeros(ang.shape[:-1] + (rest,), F32)
    z8 = jnp.zeros_like(sin)
    reps = LANES // HEAD_DIM
    flat = lambda t: jnp.tile(t, (1, 1, reps)).reshape(-1, LANES)
    cos_t = flat(jnp.concatenate([cos, cos, ones], axis=-1))
    sa_t = flat(jnp.concatenate([z8, sin, zeros], axis=-1))
    sb_t = flat(jnp.concatenate([-sin, z8, zeros], axis=-1))
    return cos_t, sa_t, sb_t


def _hybrid_mixer(h, rope, bsz, l, layer, w_in_all, sinks, conv_w, conv_b, dt_bias, a_log, ssd_d, ssd_norm,
                  lam_re, lam_im, log_step, b_re, b_im, c_re, c_im, s5_d, w_glu, b_glu, w_branch, w_out_all):
    m = h.shape[0]
    o0 = Q_W + 2 * KV_W
    o1 = o0 + SSD_INNER
    o2 = o1 + SSD_CONV_CH
    o3 = o2 + SSD_HEADS
    cast = lambda w: w.astype(BF)
    w_in = [(w_in_all, layer)]
    tm = min(1024, m)
    tab = [pl.BlockSpec((tm, LANES), lambda j, i: (i, 0))] * 3
    qkv = _mm_wres(h, w_in, _ep_qkv, n=o0, extras=rope, extra_specs=tab, name="qkv_proj_rope")
    z = _mm_wres(h, w_in, _ep_plain, n=SSD_INNER, col0=o0, name="z_proj")
    xbc = _mm_wres(h, w_in, _ep_plain, n=SSD_CONV_CH, col0=o1, name="xbc_proj")
    dt_raw = _mm_wres(h, w_in, _ep_plain, n=LANES, col0=o2, out_dtype=F32, name="dt_proj")
    w_tail = [(w_in_all[layer:layer + 1, :, o3:], 0)]
    u = _mm_wres(h, w_tail, _ep_plain, n=S5_WIDTH, name="u_proj")
    gates = _mm_wres(h, w_tail, _ep_sigmoid, n=w_in_all.shape[2] - o3 - S5_WIDTH, col0=S5_WIDTH,
                     name="gate_proj")

    o_attn = _attention(qkv, sinks, bsz, l)
    o_ssd = _ssd(z, xbc, dt_raw, conv_w, conv_b, dt_bias, a_log, ssd_d, ssd_norm, bsz, l)
    ops = _s5_operators(lam_re, lam_im, log_step, b_re, b_im, c_re, c_im, l // S5_CHUNK)
    y_s5 = _s5_scan(u.reshape(bsz, l, S5_WIDTH), ops, bsz, l)
    o_s5 = _s5_glu(y_s5, u, s5_d, cast(w_glu), b_glu)

    wb = cast(w_branch)
    merged = _branch_merge(o_attn, o_ssd, o_s5, wb[:Q_W], wb[Q_W:Q_W + SSD_INNER], wb[Q_W + SSD_INNER:], gates)
    return _mm_wres(merged, [(w_out_all, layer)], _ep_plain, n=w_out_all.shape[2], name="out_proj")


def kernel(x, c, positions, w_mod, b_mod, norm_mix_pre, norm_mix_post, norm_ffn_pre, norm_ffn_post, w_in, attn_sinks, conv_w, conv_b, dt_bias, a_log, ssd_d, ssd_norm, s5_lam_re, s5_lam_im, s5_log_step, s5_b_re, s5_b_im, s5_c_re, s5_c_im, s5_d, s5_w_glu, s5_b_glu, w_branch, w_out, ffn_w1, ffn_w3, ffn_w2, w_router, moe_w1, moe_w3, moe_w2):
    bsz, l, d = x.shape
    depth = w_mod.shape[0]
    assert l % SSD_CHUNK == 0 and l % ATTN_BLOCK == 0 and l % S5_CHUNK == 0
    rope = _rope_tables(positions)
    mod = _modulation(c, w_mod, b_mod)
    cast = lambda w: w.astype(BF)

    def mods(i):
        return [mod[i, :, k * d:(k + 1) * d] for k in range(6)]

    sh1, sc1, g1, sh2, sc2, g2 = mods(0)
    h = _pre_norm(x, norm_mix_pre[0], sh1, sc1)
    for i in range(depth):
        m = _hybrid_mixer(h.reshape(bsz * l, d), rope, bsz, l, i, w_in, attn_sinks[i], conv_w[i], conv_b[i],
                          dt_bias[i], a_log[i], ssd_d[i], ssd_norm[i], s5_lam_re[i], s5_lam_im[i],
                          s5_log_step[i], s5_b_re[i], s5_b_im[i], s5_c_re[i], s5_c_im[i], s5_d[i],
                          s5_w_glu[i], s5_b_glu[i], w_branch[i], w_out)
        is_moe = i % 2 == 1
        x, h = _post_norm(m, x, g1, norm_mix_post[i], norm_ffn_pre[i], sh2, sc2, h_dtype=F32 if is_moe else BF)
        h2 = h.reshape(bsz * l, d)
        j = i // 2
        if i % 2 == 0:
            act = _mm_wres(h2, [(ffn_w1, j), (ffn_w3, j)], _ep_swiglu, n=ffn_w1.shape[2], tn=256,
                           name="swiglu_up")
            f = _matmul(act, cast(ffn_w2[j]), tm=512, name="ffn_down")
        else:
            f = _moe(h2, w_router[j], moe_w1[j], moe_w3[j], moe_w2[j])
        if i + 1 < depth:
            sh1, sc1, g1n, sh2n, sc2n, g2n = mods(i + 1)
            x, h = _post_norm(f, x, g2, norm_ffn_post[i], norm_mix_pre[i + 1], sh1, sc1)
            g1, sh2, sc2, g2 = g1n, sh2n, sc2n, g2n
        else:
            x, _ = _post_norm(f, x, g2, norm_ffn_post[i])
    return x
```

```python
import functools
import math

import jax
import jax.numpy as jnp
from jax import lax
from jax.experimental import pallas as pl
from jax.experimental.pallas import tpu as pltpu

BF = jnp.bfloat16
F32 = jnp.float32

HEAD_DIM = 64
Q_HEADS = 32
KV_HEADS = 4
Q_GROUP = Q_HEADS // KV_HEADS
Q_W = Q_HEADS * HEAD_DIM
KV_W = KV_HEADS * HEAD_DIM
ATTN_BLOCK = 128
ROT_DIM = 16
ROT_HALF = ROT_DIM // 2
ROPE_THETA = 500000.0
SSD_INNER = 2048
SSD_HEADS = 32
SSD_HEAD_DIM = 64
SSD_GROUPS = 4
SSD_STATE = 128
SSD_CONV = 4
SSD_CHUNK = 128
SSD_CONV_CH = SSD_INNER + 2 * SSD_GROUPS * SSD_STATE
S5_WIDTH = 1536
S5_GROUP_SIZE = 16
S5_GROUPS = 96
S5_STATE = 64
S5_CHUNK = 16
S5_LANE_GROUPS = 8
N_EXPERTS = 8
RMS_EPS = 1e-6
LANES = 128
MIB = 1024 * 1024
NEG_BIG = -1e30


def _cparams(semantics, vmem_bytes):
    limit = int(min(max(vmem_bytes * 1.25 + 4 * MIB, 16 * MIB), 60 * MIB))
    return pltpu.CompilerParams(dimension_semantics=semantics, vmem_limit_bytes=limit)


def _pick(n, candidates):
    for c in candidates:
        if n % c == 0:
            return c
    raise ValueError(f"no tile in {candidates} divides {n}")


def _silu(x):
    return x * jax.nn.sigmoid(x)


def _mod_kernel(c_ref, w_ref, b_ref, o_ref):
    c = c_ref[...]
    a = _silu(c).astype(BF)
    o_ref[0] = jnp.dot(a, w_ref[0].astype(BF), preferred_element_type=F32) + b_ref[0]


def _modulation(c, w_mod, b_mod):
    depth, d, n = w_mod.shape
    bsz = c.shape[0]
    rows = 8
    c_pad = jnp.zeros((rows, d), F32).at[:bsz].set(c)
    tn = _pick(n, (512, 256, 128))
    out = pl.pallas_call(
        _mod_kernel,
        out_shape=jax.ShapeDtypeStruct((depth, rows, n), F32),
        grid=(depth, n // tn),
        in_specs=[pl.BlockSpec((rows, d), lambda l, j: (0, 0)),
                  pl.BlockSpec((1, d, tn), lambda l, j: (l, 0, j)),
                  pl.BlockSpec((1, 1, tn), lambda l, j: (l, 0, j))],
        out_specs=pl.BlockSpec((1, rows, tn), lambda l, j: (l, 0, j)),
        compiler_params=_cparams(("parallel", "arbitrary"), 2 * d * tn * 4 + d * tn * 2),
        name="adaln_modulation",
    )(c_pad, w_mod, b_mod.reshape(depth, 1, n))
    return out[:, :bsz]


def _rms(x, w):
    return x * lax.rsqrt(jnp.mean(x * x, axis=-1, keepdims=True) + RMS_EPS) * w


def _pre_norm_kernel(x_ref, w_ref, sh_ref, sc_ref, h_ref):
    h = _rms(x_ref[0], w_ref[...])
    h_ref[0] = (h * (1.0 + sc_ref[0]) + sh_ref[0]).astype(h_ref.dtype)


def _pre_norm(x, w, shift, scale):
    bsz, l, d = x.shape
    tl = _pick(l, (256, 128))
    vec = pl.BlockSpec((1, 1, d), lambda b, i: (b, 0, 0))
    return pl.pallas_call(
        _pre_norm_kernel,
        out_shape=jax.ShapeDtypeStruct((bsz, l, d), BF),
        grid=(bsz, l // tl),
        in_specs=[pl.BlockSpec((1, tl, d), lambda b, i: (b, i, 0)),
                  pl.BlockSpec((1, d), lambda b, i: (0, 0)), vec, vec],
        out_specs=pl.BlockSpec((1, tl, d), lambda b, i: (b, i, 0)),
        compiler_params=_cparams(("parallel", "parallel"), 2 * tl * d * 6 + 2 * tl * d * 4),
        name="pre_norm",
    )(x, w.reshape(1, d), shift.reshape(bsz, 1, d), scale.reshape(bsz, 1, d))


def _post_norm_kernel(m_ref, x_ref, g_ref, wpost_ref, wpre_ref, sh_ref, sc_ref, xo_ref, h_ref):
    xn = x_ref[0] + g_ref[0] * _rms(m_ref[0].astype(F32), wpost_ref[...])
    xo_ref[0] = xn
    h = _rms(xn, wpre_ref[...])
    h_ref[0] = (h * (1.0 + sc_ref[0]) + sh_ref[0]).astype(h_ref.dtype)


def _post_norm_last_kernel(m_ref, x_ref, g_ref, wpost_ref, xo_ref):
    xo_ref[0] = x_ref[0] + g_ref[0] * _rms(m_ref[0].astype(F32), wpost_ref[...])


def _post_norm(m, x, gate, w_post, w_pre=None, shift=None, scale=None, h_dtype=BF):
    bsz, l, d = x.shape
    tl = _pick(l, (256, 128))
    row = pl.BlockSpec((1, tl, d), lambda b, i: (b, i, 0))
    vec = pl.BlockSpec((1, 1, d), lambda b, i: (b, 0, 0))
    par = pl.BlockSpec((1, d), lambda b, i: (0, 0))
    m = m.reshape(bsz, l, d)
    gate = gate.reshape(bsz, 1, d)
    cp = _cparams(("parallel", "parallel"), 2 * tl * d * (2 + 4 + 4 + 2) + 3 * tl * d * 4)
    if w_pre is None:
        return pl.pallas_call(
            _post_norm_last_kernel,
            out_shape=jax.ShapeDtypeStruct((bsz, l, d), F32),
            grid=(bsz, l // tl),
            in_specs=[row, row, vec, par],
            out_specs=row,
            compiler_params=cp,
            name="post_norm_last",
        )(m, x, gate, w_post.reshape(1, d)), None
    return pl.pallas_call(
        _post_norm_kernel,
        out_shape=(jax.ShapeDtypeStruct((bsz, l, d), F32), jax.ShapeDtypeStruct((bsz, l, d), h_dtype)),
        grid=(bsz, l // tl),
        in_specs=[row, row, vec, par, par, vec, vec],
        out_specs=(row, row),
        compiler_params=cp,
        name="post_norm",
    )(m, x, gate, w_post.reshape(1, d), w_pre.reshape(1, d), shift.reshape(bsz, 1, d),
      scale.reshape(bsz, 1, d))


def _mm_call(body, a_list, w_list, extras, extra_specs, *, n, tm, tn, out_dtype, name):
    m = a_list[0].shape[0]
    tm = min(tm, m)
    assert m % tm == 0 and n % tn == 0
    a_specs = [pl.BlockSpec((tm, a.shape[1]), lambda i, j: (i, 0)) for a in a_list]
    w_specs = [pl.BlockSpec((w.shape[0], tn), lambda i, j: (0, j)) for w in w_list]
    vmem = sum(2 * tm * a.shape[1] * a.dtype.itemsize for a in a_list)
    vmem += sum(2 * w.shape[0] * tn * w.dtype.itemsize for w in w_list)
    vmem += 2 * tm * tn * 4 * (1 + len(extras)) + 3 * tm * tn * 4
    return pl.pallas_call(
        body,
        out_shape=jax.ShapeDtypeStruct((m, n), out_dtype),
        grid=(m // tm, n // tn),
        in_specs=a_specs + w_specs + list(extra_specs(tm, tn)),
        out_specs=pl.BlockSpec((tm, tn), lambda i, j: (i, j)),
        compiler_params=_cparams(("parallel", "arbitrary"), vmem),
        name=name,
    )(*a_list, *w_list, *extras)


def _dot(a, w):
    return jnp.dot(a, w, preferred_element_type=F32)


def _plain_body(a_ref, w_ref, o_ref):
    o_ref[...] = _dot(a_ref[...], w_ref[...]).astype(o_ref.dtype)


def _matmul(a, w, *, tn=None, tm=1024, out_dtype=BF, body=_plain_body, name="matmul"):
    n = w.shape[1]
    tn = tn or _pick(n, (512, 256, 128))
    return _mm_call(body, [a], [w], [], lambda tm_, tn_: [], n=n, tm=tm, tn=tn,
                    out_dtype=out_dtype, name=name)


def _wres_kernel(n_w, shift, epilogue, a_ref, *refs):
    n_in = n_w * (2 if shift else 1)
    w_refs, nxt_refs = refs[:n_w], refs[n_w:n_in]
    extras, o_ref, wb = refs[n_in:-n_w - 1], refs[-n_w - 1], refs[-n_w:]

    @pl.when(pl.program_id(1) == 0)
    def _():
        for idx, (w_ref, b) in enumerate(zip(w_refs, wb)):
            w = w_ref[0]
            if shift:
                w = jnp.concatenate([w[:, shift:], nxt_refs[idx][0][:, :shift]], axis=1)
            b[...] = w.astype(BF)

    a = a_ref[...]
    epilogue([_dot(a, b[...]) for b in wb], extras, o_ref)


def _mm_wres(a, weights, epilogue, *, n, col0=0, tn=None, extras=(), extra_specs=(), tm=1024,
             out_dtype=BF, name):
    m, k = a.shape
    tm = min(tm, m)
    shift = col0 % LANES
    base = col0 - shift
    tn = tn or next(t for t in (512, 256, 128) if n % t == 0 and base % t == 0)
    assert m % tm == 0 and n % tn == 0 and base % tn == 0
    c0, per = base // tn, tn // LANES
    w_specs = [pl.BlockSpec((1, k, tn), functools.partial(lambda lead, j, i: (lead, 0, c0 + j), lead))
               for _, lead in weights]
    if shift:
        w_specs += [pl.BlockSpec((1, k, LANES),
                                 functools.partial(lambda lead, j, i: (lead, 0, (c0 + j + 1) * per), lead))
                    for _, lead in weights]
    w_args = [w for w, _ in weights] * (2 if shift else 1)
    vmem = 2 * tm * k * 2 + len(weights) * (2 * k * (tn + LANES) * 4 + 2 * k * tn * 2) + 2 * tm * tn * 4
    vmem += 4 * tm * tn * 4
    return pl.pallas_call(
        functools.partial(_wres_kernel, len(weights), shift, epilogue),
        out_shape=jax.ShapeDtypeStruct((m, n), out_dtype),
        grid=(n // tn, m // tm),
        in_specs=[pl.BlockSpec((tm, k), lambda j, i: (i, 0))] + w_specs + list(extra_specs),
        out_specs=pl.BlockSpec((tm, tn), lambda j, i: (i, j)),
        scratch_shapes=[pltpu.VMEM((k, tn), BF)] * len(weights),
        compiler_params=_cparams(("arbitrary", "arbitrary"), vmem),
        name=name,
    )(a, *w_args, *extras)


def _ep_plain(accs, extras, o_ref):
    o_ref[...] = accs[0].astype(o_ref.dtype)


def _ep_sigmoid(accs, extras, o_ref):
    o_ref[...] = jax.nn.sigmoid(accs[0]).astype(o_ref.dtype)


def _ep_swiglu(accs, extras, o_ref):
    o_ref[...] = (_silu(accs[0]) * accs[1]).astype(o_ref.dtype)


def _ep_qkv(accs, extras, o_ref):
    cos_ref, sa_ref, sb_ref = extras
    j = pl.program_id(0)
    acc = accs[0]
    tn = acc.shape[1]
    n_tiles = (Q_W + 2 * KV_W) // tn
    n_q_tiles = Q_W // tn

    def emit(tile, scale):
        cols = []
        for c in range(tn // LANES):
            r = acc[:, c * LANES:(c + 1) * LANES]
            if tile * tn + c * LANES < Q_W + KV_W:
                r = (r * cos_ref[...] + pltpu.roll(r, ROT_HALF, axis=1) * sa_ref[...]
                     + pltpu.roll(r, LANES - ROT_HALF, axis=1) * sb_ref[...])
            cols.append(r)
        out = jnp.concatenate(cols, axis=1)
        o_ref[...] = (out * scale if scale != 1.0 else out).astype(o_ref.dtype)

    @pl.when(j < n_q_tiles)
    def _():
        emit(0, HEAD_DIM ** -0.5)

    for tile in range(n_q_tiles, n_tiles):
        @pl.when(j == tile)
        def _(tile=tile):
            emit(tile, 1.0)


def _merge_body(oa_ref, ob_ref, oc_ref, wa_ref, wb_ref, wc_ref, ga_ref, gb_ref, gc_ref, o_ref):
    acc = ga_ref[...].astype(F32) * _dot(oa_ref[...], wa_ref[...])
    acc += gb_ref[...].astype(F32) * _dot(ob_ref[...], wb_ref[...])
    acc += gc_ref[...].astype(F32) * _dot(oc_ref[...], wc_ref[...])
    o_ref[...] = acc.astype(o_ref.dtype)


def _branch_merge(o_attn, o_ssd, o_s5, wa, wb, wc, gates):
    d = wa.shape[1]
    tn = _pick(d, (512, 256, 128))
    nt = d // tn

    def gate_specs(tm_, tn_):
        return [pl.BlockSpec((tm_, tn_), functools.partial(lambda br, i, j: (i, br * nt + j), br))
                for br in range(3)]

    return _mm_call(_merge_body, [o_attn, o_ssd, o_s5], [wa, wb, wc], [gates, gates, gates],
                    gate_specs, n=d, tm=512, tn=tn, out_dtype=BF, name="branch_merge")


def _attn_kernel(sink_ref, q_ref, kc_ref, kp_ref, vc_ref, vp_ref, o_ref):
    n = pl.program_id(1)
    qi = lax.broadcasted_iota(jnp.int32, (ATTN_BLOCK, 2 * ATTN_BLOCK), 0)
    sj = lax.broadcasted_iota(jnp.int32, (ATTN_BLOCK, 2 * ATTN_BLOCK), 1)
    rel = qi + ATTN_BLOCK - sj
    valid = (rel >= 0) & (rel < ATTN_BLOCK) & ((sj >= ATTN_BLOCK) | (n > 0))
    q = q_ref[0]
    kv = 2 * ATTN_BLOCK
    pair_w = 2 * HEAD_DIM
    lane = lax.broadcasted_iota(jnp.int32, (ATTN_BLOCK, pair_w), 1)
    first = lane < HEAD_DIM
    zeros = jnp.zeros((kv, HEAD_DIM), BF)
    ones = jnp.ones((kv, HEAD_DIM), BF)
    outs = []
    for g in range(KV_HEADS):
        sl = slice(g * HEAD_DIM, (g + 1) * HEAD_DIM)
        kbt = jnp.concatenate([kp_ref[0, g], kc_ref[0, g]], axis=1)
        vb = jnp.concatenate([vp_ref[0][:, sl], vc_ref[0][:, sl]], axis=0)
        pv = jnp.concatenate([jnp.concatenate([vb, zeros, ones, zeros], axis=1),
                              jnp.concatenate([zeros, vb, zeros, ones], axis=1)], axis=0)
        for r in range(0, Q_GROUP, 2):
            ps, sinks = [], []
            for k in range(2):
                hd = g * Q_GROUP + r + k
                s = _dot(q[:, hd * HEAD_DIM:(hd + 1) * HEAD_DIM], kbt)
                s = jnp.where(valid, s, NEG_BIG)
                mx = jnp.maximum(jnp.max(s, axis=-1, keepdims=True), sink_ref[hd])
                ps.append(jnp.exp(s - mx).astype(BF))
                sinks.append(jnp.exp(sink_ref[hd] - mx))
            res = _dot(jnp.concatenate(ps, axis=1), pv)
            den = res[:, pair_w:] + jnp.where(first, sinks[0], sinks[1])
            outs.append(res[:, :pair_w] / den)
    o_ref[0] = jnp.concatenate(outs, axis=1).astype(o_ref.dtype)


def _attention(qkv, sinks, bsz, l):
    nb = l // ATTN_BLOCK
    qkv = qkv.reshape(bsz, l, Q_W + 2 * KV_W)
    k_t = qkv[:, :, Q_W:Q_W + KV_W].reshape(bsz, l, KV_HEADS, HEAD_DIM).transpose(0, 2, 3, 1)
    vcol = Q_W // KV_W + 1
    kspec = lambda idx: pl.BlockSpec((1, KV_HEADS, HEAD_DIM, ATTN_BLOCK), lambda b, n: (b, 0, 0, idx(n)))
    vspec = lambda idx: pl.BlockSpec((1, ATTN_BLOCK, KV_W), lambda b, n: (b, idx(n), vcol))
    cur = lambda n: n
    prev = lambda n: jnp.maximum(n - 1, 0)
    out = pl.pallas_call(
        _attn_kernel,
        out_shape=jax.ShapeDtypeStruct((bsz, l, Q_W), BF),
        grid=(bsz, nb),
        in_specs=[pl.BlockSpec(memory_space=pltpu.SMEM),
                  pl.BlockSpec((1, ATTN_BLOCK, Q_W), lambda b, n: (b, n, 0)),
                  kspec(cur), kspec(prev), vspec(cur), vspec(prev)],
        out_specs=pl.BlockSpec((1, ATTN_BLOCK, Q_W), lambda b, n: (b, n, 0)),
        compiler_params=_cparams(("parallel", "parallel"), 8 * MIB),
        name="sliding_window_attention",
    )(sinks.astype(F32), qkv, k_t, k_t, qkv, qkv)
    return out.reshape(bsz * l, Q_W)


def _split_dot(f, e):
    hi = f.astype(BF)
    lo = (f - hi.astype(F32)).astype(BF)
    return _dot(hi, e) + _dot(lo, e)


def _ssd_kernel(z_ref, xbc_ref, dt_ref, cw_ref, cb_ref, dtb_ref, alog_ref, dsk_ref, nw_ref,
                o_ref, ext_ref, state_ref):
    c = pl.program_id(1)
    q = SSD_CHUNK
    halo = 8

    @pl.when(c == 0)
    def _():
        ext_ref[0:halo, :] = jnp.zeros((halo, SSD_CONV_CH), F32)
        state_ref[...] = jnp.zeros_like(state_ref)

    ext_ref[halo:halo + q, :] = xbc_ref[0].astype(F32)
    conv = cb_ref[...] + cw_ref[SSD_CONV - 1:SSD_CONV, :] * ext_ref[halo:halo + q, :]
    for j in range(SSD_CONV - 1):
        off = halo - (SSD_CONV - 1) + j
        conv = conv + cw_ref[j:j + 1, :] * ext_ref[off:off + q, :]
    tail = ext_ref[q:q + halo, :]
    ext_ref[0:halo, :] = tail
    xbc = _silu(conv)
    xs = xbc[:, :SSD_INNER]
    xs_bf = xs.astype(BF)
    bm = xbc[:, SSD_INNER:SSD_INNER + SSD_GROUPS * SSD_STATE].astype(BF)
    cm = xbc[:, SSD_INNER + SSD_GROUPS * SSD_STATE:].astype(BF)

    dt = jax.nn.softplus(dt_ref[0] + dtb_ref[...])
    a = -jnp.exp(alog_ref[...])
    cs = dt * a
    row = lax.broadcasted_iota(jnp.int32, (q, LANES), 0)
    sh = 1
    while sh < q:
        cs = cs + jnp.where(row >= sh, pltpu.roll(cs, sh, axis=0), 0.0)
        sh *= 2
    cs_last = cs[q - 1:q, :]
    ecs = jnp.exp(cs)
    dte = jnp.exp(cs_last - cs) * dt
    cs_t = cs.T
    dt_t = dt.T

    hl = lax.broadcasted_iota(jnp.int32, (LANES, SSD_INNER), 0)
    hc = lax.broadcasted_iota(jnp.int32, (LANES, SSD_INNER), 1)
    expand = jnp.where(hl == hc // SSD_HEAD_DIM, 1.0, 0.0).astype(BF)
    ecs_x = _split_dot(ecs, expand)
    dte_x = _split_dot(dte, expand)

    li = lax.broadcasted_iota(jnp.int32, (q, q), 0)
    si = lax.broadcasted_iota(jnp.int32, (q, q), 1)
    causal = li >= si
    heads_per_group = SSD_HEADS // SSD_GROUPS
    gw = heads_per_group * SSD_HEAD_DIM
    y_parts = []
    for g in range(SSD_GROUPS):
        ns = slice(g * SSD_STATE, (g + 1) * SSD_STATE)
        gs = slice(g * gw, (g + 1) * gw)
        bm_g, cm_g = bm[:, ns], cm[:, ns]
        cb = lax.dot_general(cm_g, bm_g, (((1,), (1,)), ((), ())), preferred_element_type=F32)
        for r in range(heads_per_group):
            hd = g * heads_per_group + r
            seg = cs[:, hd:hd + 1] - cs_t[hd:hd + 1, :]
            decay = jnp.exp(jnp.where(causal, seg, NEG_BIG))
            w = (cb * decay * dt_t[hd:hd + 1, :]).astype(BF)
            y_parts.append(_dot(w, xs_bf[:, hd * SSD_HEAD_DIM:(hd + 1) * SSD_HEAD_DIM]))
    y = jnp.concatenate(y_parts, axis=1)
    xw = (xs * dte_x).astype(BF)
    for g in range(SSD_GROUPS):
        ns = slice(g * SSD_STATE, (g + 1) * SSD_STATE)
        gs = slice(g * gw, (g + 1) * gw)
        prev = state_ref[:, gs]
        y_off = _dot(cm[:, ns], prev.astype(BF)) * ecs_x[:, gs]
        upd = lax.dot_general(bm[:, ns], xw[:, gs], (((0,), (0,)), ((), ())),
                              preferred_element_type=F32)
        state_ref[:, gs] = prev * ecs_x[q - 1:q, gs] + upd
        yg = y[:, gs] + y_off + dsk_ref[:, gs] * xs[:, gs]
        yg = yg * _silu(z_ref[0, :, gs].astype(F32))
        yg = yg * lax.rsqrt(jnp.mean(yg * yg, axis=-1, keepdims=True) + RMS_EPS)
        o_ref[0, :, gs] = (yg * nw_ref[:, gs]).astype(o_ref.dtype)


def _ssd(z, xbc, dt_raw, conv_w, conv_b, dt_bias, a_log, d_skip, norm_w, bsz, l):
    q = SSD_CHUNK
    pad = LANES - SSD_HEADS
    row = lambda w: pl.BlockSpec((1, q, w), lambda b, c: (b, c, 0))
    par = lambda r, w: pl.BlockSpec((r, w), lambda b, c: (0, 0))
    out = pl.pallas_call(
        _ssd_kernel,
        out_shape=jax.ShapeDtypeStruct((bsz, l, SSD_INNER), BF),
        grid=(bsz, l // q),
        in_specs=[row(SSD_INNER), row(SSD_CONV_CH), row(LANES),
                  par(SSD_CONV, SSD_CONV_CH), par(1, SSD_CONV_CH), par(1, LANES), par(1, LANES),
                  par(1, SSD_INNER), par(1, SSD_INNER)],
        out_specs=row(SSD_INNER),
        scratch_shapes=[pltpu.VMEM((q + 8, SSD_CONV_CH), F32), pltpu.VMEM((SSD_STATE, SSD_INNER), F32)],
        compiler_params=_cparams(("parallel", "arbitrary"), 24 * MIB),
        name="ssd_chunked",
    )(z.reshape(bsz, l, SSD_INNER), xbc.reshape(bsz, l, SSD_CONV_CH), dt_raw.reshape(bsz, l, LANES),
      conv_w, conv_b.reshape(1, -1), jnp.pad(dt_bias, (0, pad)).reshape(1, LANES),
      jnp.pad(a_log, (0, pad)).reshape(1, LANES),
      jnp.repeat(d_skip, SSD_HEAD_DIM).reshape(1, SSD_INNER), norm_w.reshape(1, SSD_INNER))
    return out.reshape(bsz * l, SSD_INNER)


def _s5_operators(lam_re, lam_im, log_step, b_re, b_im, c_re, c_im, n_chunks):
    hp = lax.Precision.HIGHEST
    qn = S5_CHUNK
    lr, li = lam_re.astype(F32), lam_im.astype(F32)
    step = jnp.exp(log_step.astype(F32))[:, None]
    mag = jnp.exp(lr * step)
    a_re, a_im = mag * jnp.cos(li * step), mag * jnp.sin(li * step)
    den = lr * lr + li * li
    coef_re = ((a_re - 1.0) * lr + a_im * li) / den
    coef_im = (a_im * lr - (a_re - 1.0) * li) / den
    br, bi = b_re.astype(F32), b_im.astype(F32)
    bb_re = coef_re[..., None] * br - coef_im[..., None] * bi
    bb_im = coef_re[..., None] * bi + coef_im[..., None] * br

    def power(d):
        d = d.astype(F32)[None, :, None]
        m = jnp.exp(lr[:, None, :] * step[:, None, :] * d)
        ang = li[:, None, :] * step[:, None, :] * d
        return m * jnp.cos(ang), m * jnp.sin(ang)

    p_re, p_im = power(jnp.arange(qn + 1))
    cr, ci = c_re.astype(F32), c_im.astype(F32)
    m_re = cr[:, None] * p_re[:, :, None, :] - ci[:, None] * p_im[:, :, None, :]
    m_im = cr[:, None] * p_im[:, :, None, :] + ci[:, None] * p_re[:, :, None, :]
    kern = (jnp.einsum('gdip,gpj->gdij', m_re, bb_re, precision=hp)
            - jnp.einsum('gdip,gpj->gdij', m_im, bb_im, precision=hp))
    gl, nlg, gs, ps = S5_LANE_GROUPS, S5_GROUPS // S5_LANE_GROUPS, S5_GROUP_SIZE, S5_STATE
    eye = jnp.eye(gl, dtype=F32)
    lag_op = jnp.einsum('lgdij,gh->lgjdhi', kern[:, :qn].reshape(nlg, gl, qn, gs, gs), eye)
    lag_op = lag_op.reshape(nlg, LANES, qn * LANES)
    e_re, e_im = p_re[:, qn - 1 - jnp.arange(qn)], p_im[:, qn - 1 - jnp.arange(qn)]
    bt_re, bt_im = bb_re.transpose(0, 2, 1)[:, None], bb_im.transpose(0, 2, 1)[:, None]
    be = jnp.stack([e_re[:, :, None, :] * bt_re - e_im[:, :, None, :] * bt_im,
                    e_re[:, :, None, :] * bt_im + e_im[:, :, None, :] * bt_re], axis=3)
    b_end = jnp.einsum('lgsjrp,gh->lsgjrhp', be.reshape(nlg, gl, qn, gs, 2, ps), eye)
    b_end = b_end.reshape(nlg, qn * LANES, 2 * gl * ps)
    ci = jnp.stack([m_re[:, 1:].transpose(0, 3, 1, 2), -m_im[:, 1:].transpose(0, 3, 1, 2)], axis=1)
    c_in = jnp.einsum('lgrpti,gh->lrgpthi', ci.reshape(nlg, gl, 2, ps, qn, gs), eye)
    c_in = c_in.reshape(nlg, 2 * gl * ps, qn * LANES)
    n_steps = max(1, (n_chunks - 1).bit_length())
    s_re, s_im = power(qn * (2 ** jnp.arange(n_steps)))
    lanes = lambda t: t.reshape(nlg, gl, n_steps, ps).transpose(0, 2, 1, 3).reshape(nlg, n_steps, gl * ps)
    pw1 = jnp.concatenate([lanes(s_re), lanes(s_re)], axis=-1)
    pw2 = jnp.concatenate([-lanes(s_im), lanes(s_im)], axis=-1)
    return lag_op.astype(BF), b_end.astype(BF), c_in.astype(BF), pw1, pw2


def _s5_kernel(u_ref, lag_ref, bend_ref, cin_ref, pw1_ref, pw2_ref, y_ref, toep_ref, ucat_ref):
    qn = S5_CHUNK
    n_chunks = u_ref.shape[1] // qn
    n_steps = pw1_ref.shape[1]
    half = pw1_ref.shape[2] // 2

    @pl.when(pl.program_id(1) == 0)
    def _():
        toep_ref[...] = jnp.zeros_like(toep_ref)
        for s in range(qn):
            toep_ref[s * LANES:(s + 1) * LANES, s * LANES:] = lag_ref[0, :, :(qn - s) * LANES]

    for s in range(qn):
        ucat_ref[:, s * LANES:(s + 1) * LANES] = u_ref[0, pl.ds(s, n_chunks, stride=qn), :].astype(BF)
    ucat = ucat_ref[...]
    x = _dot(ucat, bend_ref[0])
    row = lax.broadcasted_iota(jnp.int32, x.shape, 0)
    for k in range(n_steps):
        sh = 1 << k
        xs = jnp.where(row >= sh, pltpu.roll(x, sh, axis=0), 0.0)
        x = x + pw1_ref[0, k:k + 1, :] * xs + pw2_ref[0, k:k + 1, :] * pltpu.roll(xs, half, axis=1)
    x_in = jnp.where(row >= 1, pltpu.roll(x, 1, axis=0), 0.0)
    y = _dot(ucat, toep_ref[...]) + _dot(x_in.astype(BF), cin_ref[0])
    for t in range(qn):
        y_ref[0, pl.ds(t, n_chunks, stride=qn), :] = y[:, t * LANES:(t + 1) * LANES]


def _s5_scan(u, ops, bsz, l):
    lag_op, b_end, c_in, pw1, pw2 = ops
    qn = S5_CHUNK
    nlg = S5_WIDTH // LANES
    nc = l // qn
    wide = qn * LANES
    st = b_end.shape[2]
    wspec = lambda r, c: pl.BlockSpec((1, r, c), lambda g, b: (g, 0, 0))
    seq = pl.BlockSpec((1, l, LANES), lambda g, b: (b, 0, g))
    y = pl.pallas_call(
        _s5_kernel,
        out_shape=jax.ShapeDtypeStruct((bsz, l, S5_WIDTH), F32),
        grid=(nlg, bsz),
        in_specs=[seq, wspec(LANES, wide), wspec(wide, st), wspec(st, wide),
                  wspec(pw1.shape[1], st), wspec(pw1.shape[1], st)],
        out_specs=seq,
        scratch_shapes=[pltpu.VMEM((wide, wide), BF), pltpu.VMEM((nc, wide), BF)],
        compiler_params=_cparams(("arbitrary", "arbitrary"),
                                 4 * l * LANES * 4 + wide * wide * 2 + 4 * wide * st * 2 + 2 * LANES * wide * 2
                                 + nc * wide * 2 + 3 * nc * wide * 4 + 4 * nc * st * 4),
        name="s5_chunked_scan",
    )(u, lag_op, b_end, c_in, pw1, pw2)
    return y.reshape(bsz * l, S5_WIDTH)


def _s5_glu_kernel(y_ref, u_ref, d_ref, w_ref, b_ref, o_ref):
    y = y_ref[...].astype(F32) + d_ref[...] * u_ref[...].astype(F32)
    v = jax.nn.gelu(y, approximate=True)
    o_ref[...] = (v * jax.nn.sigmoid(_dot(v.astype(BF), w_ref[...]) + b_ref[...])).astype(o_ref.dtype)


def _s5_glu(y, u, d_skip, w_glu, b_glu):
    m, w = y.shape
    tm = _pick(m, (512, 256, 128))
    row = pl.BlockSpec((tm, w), lambda i: (i, 0))
    par = pl.BlockSpec((1, w), lambda i: (0, 0))
    return pl.pallas_call(
        _s5_glu_kernel,
        out_shape=jax.ShapeDtypeStruct((m, w), BF),
        grid=(m // tm,),
        in_specs=[row, row, par, pl.BlockSpec((w, w), lambda i: (0, 0)), par],
        out_specs=row,
        compiler_params=_cparams(("parallel",), 4 * tm * w * 4 + 2 * tm * w * 2 + 2 * w * w * 2 + 4 * tm * w * 4),
        name="s5_gelu_glu",
    )(y, u, d_skip.reshape(1, w), w_glu, b_glu.reshape(1, w))


MOE_ROW_TILE = 256
INFO_E1, INFO_E2, INFO_W1, INFO_W2, INFO_R1, INFO_R2 = range(6)


def _split_bf16(x):
    hi = x.astype(BF)
    return hi, (x - hi.astype(F32)).astype(BF)


def _router_kernel(h_ref, w_ref, info_ref, cnt_ref, carry_ref):
    i = pl.program_id(0)

    @pl.when(i == 0)
    def _():
        carry_ref[...] = jnp.zeros_like(carry_ref)

    hh, hl = _split_bf16(h_ref[...])
    wh, wl = _split_bf16(w_ref[...])
    logits = _dot(hh, wh) + _dot(hl, wh) + _dot(hh, wl)
    tm = logits.shape[0]
    lane = lax.broadcasted_iota(jnp.int32, logits.shape, 1)
    lg = jnp.where(lane < N_EXPERTS, logits, -jnp.inf)
    m1 = jnp.max(lg, axis=-1, keepdims=True)
    i1 = jnp.min(jnp.where(lg == m1, lane, LANES), axis=-1, keepdims=True)
    lg2 = jnp.where(lane == i1, -jnp.inf, lg)
    m2 = jnp.max(lg2, axis=-1, keepdims=True)
    i2 = jnp.min(jnp.where(lg2 == m2, lane, LANES), axis=-1, keepdims=True)
    e = jnp.exp(m2 - m1)
    w1 = 1.0 / (1.0 + e)
    w2 = e * w1
    onehot = jnp.where(lane == i1, 1.0, 0.0) + jnp.where(lane == i2, 1.0, 0.0)
    rt = lax.broadcasted_iota(jnp.int32, (tm, tm), 0)
    ct = lax.broadcasted_iota(jnp.int32, (tm, tm), 1)
    before = jnp.where(rt > ct, 1.0, 0.0).astype(BF)
    rank = _dot(before, onehot.astype(BF)) + carry_ref[...]
    r1 = jnp.sum(jnp.where(lane == i1, rank, 0.0), axis=-1, keepdims=True)
    r2 = jnp.sum(jnp.where(lane == i2, rank, 0.0), axis=-1, keepdims=True)
    carry_ref[...] += jnp.sum(onehot, axis=0, keepdims=True)
    cnt_ref[...] = carry_ref[...]
    info = jnp.zeros_like(logits)
    for slot, val in ((INFO_E1, i1.astype(F32)), (INFO_E2, i2.astype(F32)), (INFO_W1, w1), (INFO_W2, w2),
                      (INFO_R1, r1), (INFO_R2, r2)):
        info = jnp.where(lane == slot, val, info)
    info_ref[...] = info


def _router(h, w_router):
    m, d = h.shape
    tm = _pick(m, (512, 256, 128))
    w_pad = jnp.zeros((d, LANES), F32).at[:, :N_EXPERTS].set(w_router.astype(F32))
    return pl.pallas_call(
        _router_kernel,
        out_shape=(jax.ShapeDtypeStruct((m, LANES), F32), jax.ShapeDtypeStruct((1, LANES), F32)),
        grid=(m // tm,),
        in_specs=[pl.BlockSpec((tm, d), lambda i: (i, 0)), pl.BlockSpec((d, LANES), lambda i: (0, 0))],
        out_specs=(pl.BlockSpec((tm, LANES), lambda i: (i, 0)), pl.BlockSpec((1, LANES), lambda i: (0, 0))),
        scratch_shapes=[pltpu.VMEM((1, LANES), F32)],
        compiler_params=_cparams(("arbitrary",), 2 * tm * d * 4 + 2 * tm * d * 2 + 2 * d * LANES * 4
                                 + 2 * tm * tm * 2 + 8 * tm * LANES * 4),
        name="moe_router",
    )(h, w_pad)


def _row_copy(src_hbm, src_row, buf, slot, dst_row, sem):
    return pltpu.make_async_copy(src_hbm.at[pl.ds(src_row, 1)], buf.at[slot, pl.ds(dst_row, 1)], sem.at[slot])


def _dispatch_kernel(tok_ref, nu_ref, h_hbm, o_ref, buf, sem):
    i = pl.program_id(0)
    rows = o_ref.shape[0]
    nu = nu_ref[0]

    def issue(tile, slot):
        def body(r, carry):
            _row_copy(h_hbm, tok_ref[tile * rows + r], buf, slot, r, sem).start()
            return carry
        lax.fori_loop(0, rows, body, 0, unroll=8)

    def wait(slot):
        def body(r, carry):
            _row_copy(h_hbm, 0, buf, slot, 0, sem).wait()
            return carry
        lax.fori_loop(0, rows, body, 0, unroll=8)

    @pl.when(i == 0)
    def _():
        issue(0, 0)

    @pl.when(i + 1 < nu)
    def _():
        issue(i + 1, (i + 1) % 2)

    @pl.when(i < nu)
    def _():
        wait(i % 2)
        o_ref[...] = buf[i % 2].astype(o_ref.dtype)

    @pl.when(i >= nu)
    def _():
        o_ref[...] = jnp.zeros_like(o_ref)


def _dispatch(h, tok_of_row, n_used, n_rows):
    d = h.shape[1]
    rt = MOE_ROW_TILE
    return pl.pallas_call(
        _dispatch_kernel,
        out_shape=jax.ShapeDtypeStruct((n_rows, d), BF),
        grid_spec=pltpu.PrefetchScalarGridSpec(
            num_scalar_prefetch=2,
            grid=(n_rows // rt,),
            in_specs=[pl.BlockSpec(memory_space=pl.ANY)],
            out_specs=pl.BlockSpec((rt, d), lambda i, tok, nu: (i, 0)),
            scratch_shapes=[pltpu.VMEM((2, rt, d), F32), pltpu.SemaphoreType.DMA((2,))]),
        compiler_params=_cparams(("arbitrary",), 2 * rt * d * 4 + 2 * rt * d * 2 + rt * d * 4),
        name="moe_dispatch",
    )(tok_of_row, n_used, h)


def _combine_kernel(p1_ref, p2_ref, y_hbm, o_ref, buf, sem):
    i = pl.program_id(0)
    n = pl.num_programs(0)
    rows = o_ref.shape[0]

    def issue(tile, slot):
        def body(r, carry):
            t = tile * rows + r
            _row_copy(y_hbm, p1_ref[t], buf, slot, r, sem).start()
            _row_copy(y_hbm, p2_ref[t], buf, slot, rows + r, sem).start()
            return carry
        lax.fori_loop(0, rows, body, 0, unroll=4)

    def wait(slot):
        def body(r, carry):
            _row_copy(y_hbm, 0, buf, slot, 0, sem).wait()
            return carry
        lax.fori_loop(0, 2 * rows, body, 0, unroll=8)

    @pl.when(i == 0)
    def _():
        issue(0, 0)

    @pl.when(i + 1 < n)
    def _():
        issue(i + 1, (i + 1) % 2)

    wait(i % 2)
    both = buf[i % 2]
    o_ref[...] = (both[:rows] + both[rows:]).astype(o_ref.dtype)


def _combine(y_rows, pos1, pos2, n_tokens):
    d = y_rows.shape[1]
    tt = 128
    return pl.pallas_call(
        _combine_kernel,
        out_shape=jax.ShapeDtypeStruct((n_tokens, d), BF),
        grid_spec=pltpu.PrefetchScalarGridSpec(
            num_scalar_prefetch=2,
            grid=(n_tokens // tt,),
            in_specs=[pl.BlockSpec(memory_space=pl.ANY)],
            out_specs=pl.BlockSpec((tt, d), lambda i, p1, p2: (i, 0)),
            scratch_shapes=[pltpu.VMEM((2, 2 * tt, d), F32), pltpu.SemaphoreType.DMA((2,))]),
        compiler_params=_cparams(("arbitrary",), 4 * tt * d * 4 + 2 * tt * d * 2 + 2 * tt * d * 4),
        name="moe_combine",
    )(pos1, pos2, y_rows)


def _expert_changed(te_ref, i):
    return jnp.logical_or(i == 0, te_ref[i] != te_ref[jnp.maximum(i - 1, 0)])


def _moe_up_kernel(te_ref, nu_ref, a_ref, w1_ref, w3_ref, rw_ref, o_ref, w1b, w3b):
    i = pl.program_id(1)
    valid = i < nu_ref[0]

    @pl.when(jnp.logical_and(valid, _expert_changed(te_ref, i)))
    def _():
        w1b[...] = w1_ref[0].astype(BF)
        w3b[...] = w3_ref[0].astype(BF)

    @pl.when(valid)
    def _():
        a = a_ref[...]
        act = _silu(_dot(a, w1b[...])) * _dot(a, w3b[...])
        o_ref[...] = (act * rw_ref[...]).astype(o_ref.dtype)

    @pl.when(jnp.logical_not(valid))
    def _():
        o_ref[...] = jnp.zeros_like(o_ref)


def _moe_down_kernel(te_ref, nu_ref, a_ref, w_ref, o_ref, wb):
    i = pl.program_id(1)
    valid = i < nu_ref[0]

    @pl.when(jnp.logical_and(valid, _expert_changed(te_ref, i)))
    def _():
        wb[...] = w_ref[0].astype(BF)

    @pl.when(valid)
    def _():
        o_ref[...] = _dot(a_ref[...], wb[...]).astype(o_ref.dtype)

    @pl.when(jnp.logical_not(valid))
    def _():
        o_ref[...] = jnp.zeros_like(o_ref)


def _grouped_call(kern, rows_in, weights, extras, *, tn, out_dtype, tile_expert, n_used, name):
    n_rows, k = rows_in.shape
    n = weights[0].shape[2]
    rt = MOE_ROW_TILE
    last = lambda i, nu: jnp.minimum(i, nu[0] - 1)
    row_spec = lambda w: pl.BlockSpec((rt, w), lambda j, i, te, nu: (last(i, nu), 0))
    w_spec = pl.BlockSpec((1, k, tn), lambda j, i, te, nu: (te[last(i, nu)], 0, j))
    vmem = 2 * rt * k * 2 + len(weights) * (2 * k * tn * 4 + k * tn * 2) + 2 * rt * tn * 4 + 4 * rt * tn * 4
    return pl.pallas_call(
        kern,
        out_shape=jax.ShapeDtypeStruct((n_rows, n), out_dtype),
        grid_spec=pltpu.PrefetchScalarGridSpec(
            num_scalar_prefetch=2,
            grid=(n // tn, n_rows // rt),
            in_specs=[row_spec(k)] + [w_spec] * len(weights) + [row_spec(e.shape[1]) for e in extras],
            out_specs=pl.BlockSpec((rt, tn), lambda j, i, te, nu: (i, j)),
            scratch_shapes=[pltpu.VMEM((k, tn), BF)] * len(weights)),
        compiler_params=_cparams(("arbitrary", "arbitrary"), vmem),
        name=name,
    )(tile_expert, n_used, rows_in, *weights, *extras)


def _moe(h, w_router, w1, w3, w2):
    m, d = h.shape
    ne, _, f = w1.shape
    rt = MOE_ROW_TILE
    n_rows = 2 * m + ne * rt
    info, counts = _router(h, w_router)
    e1, e2 = info[:, INFO_E1].astype(jnp.int32), info[:, INFO_E2].astype(jnp.int32)
    r1, r2 = info[:, INFO_R1].astype(jnp.int32), info[:, INFO_R2].astype(jnp.int32)
    tiles = (counts[0, :ne].astype(jnp.int32) + rt - 1) // rt
    cum_tiles = jnp.cumsum(tiles)
    base = (cum_tiles - tiles) * rt
    pos1, pos2 = base[e1] + r1, base[e2] + r2
    tok = jnp.arange(m, dtype=jnp.int32)
    tok_of_row = jnp.zeros((n_rows,), jnp.int32).at[pos1].set(tok).at[pos2].set(tok)
    row_w = jnp.zeros((n_rows,), F32).at[pos1].set(info[:, INFO_W1]).at[pos2].set(info[:, INFO_W2])
    n_used = cum_tiles[-1:].astype(jnp.int32)
    tile_expert = jnp.minimum(jnp.searchsorted(cum_tiles, jnp.arange(n_rows // rt, dtype=jnp.int32),
                                               side='right'), ne - 1).astype(jnp.int32)
    rows = _dispatch(h, tok_of_row, n_used, n_rows)
    grouped = functools.partial(_grouped_call, tile_expert=tile_expert, n_used=n_used)
    act = grouped(_moe_up_kernel, rows, [w1, w3], [row_w.reshape(n_rows, 1)],
                  tn=_pick(f, (512, 256, 128)), out_dtype=BF, name="moe_up")
    y_rows = grouped(_moe_down_kernel, act, [w2], [], tn=_pick(d, (1024, 512, 256, 128)), out_dtype=F32,
                     name="moe_down")
    return _combine(y_rows, pos1, pos2, m)


def _rope_tables(positions):
    inv_freq = ROPE_THETA ** (-jnp.arange(0, ROT_DIM, 2, dtype=F32) / ROT_DIM)
    ang = positions.astype(F32)[..., None] * inv_freq
    cos, sin = jnp.cos(ang), jnp.sin(ang)
    rest = HEAD_DIM - ROT_DIM
    ones = jnp.ones(ang.shape[:-1] + (rest,), F32)
    zeros = jnp.zeros(ang.shape[:-1] + (rest,), F32)
    z8 = jnp.zeros_like(sin)
    reps = LANES // HEAD_DIM
    flat = lambda t: jnp.tile(t, (1, 1, reps)).reshape(-1, LANES)
    cos_t = flat(jnp.concatenate([cos, cos, ones], axis=-1))
    sa_t = flat(jnp.concatenate([z8, sin, zeros], axis=-1))
    sb_t = flat(jnp.concatenate([-sin, z8, zeros], axis=-1))
    return cos_t, sa_t, sb_t


def _hybrid_mixer(h, rope, bsz, l, layer, w_in_all, sinks, conv_w, conv_b, dt_bias, a_log, ssd_d, ssd_norm,
                  lam_re, lam_im, log_step, b_re, b_im, c_re, c_im, s5_d, w_glu, b_glu, w_branch, w_out_all):
    m = h.shape[0]
    o0 = Q_W + 2 * KV_W
    o1 = o0 + SSD_INNER
    o2 = o1 + SSD_CONV_CH
    o3 = o2 + SSD_HEADS
    cast = lambda w: w.astype(BF)
    w_in = [(w_in_all, layer)]
    tm = min(1024, m)
    tab = [pl.BlockSpec((tm, LANES), lambda j, i: (i, 0))] * 3
    qkv = _mm_wres(h, w_in, _ep_qkv, n=o0, extras=rope, extra_specs=tab, name="qkv_proj_rope")
    z = _mm_wres(h, w_in, _ep_plain, n=SSD_INNER, col0=o0, name="z_proj")
    xbc = _mm_wres(h, w_in, _ep_plain, n=SSD_CONV_CH, col0=o1, name="xbc_proj")
    dt_raw = _mm_wres(h, w_in, _ep_plain, n=LANES, col0=o2, out_dtype=F32, name="dt_proj")
    o4 = o3 + S5_WIDTH
    u = _mm_wres(h, w_in, _ep_plain, n=S5_WIDTH, col0=o3, out_dtype=F32, name="u_proj")
    gates = _mm_wres(h, w_in, _ep_sigmoid, n=w_in_all.shape[2] - o4, col0=o4, name="gate_proj")

    o_attn = _attention(qkv, sinks, bsz, l)
    o_ssd = _ssd(z, xbc, dt_raw, conv_w, conv_b, dt_bias, a_log, ssd_d, ssd_norm, bsz, l)
    ops = _s5_operators(lam_re, lam_im, log_step, b_re, b_im, c_re, c_im, l // S5_CHUNK)
    y_s5 = _s5_scan(u.reshape(bsz, l, S5_WIDTH), ops, bsz, l)
    o_s5 = _s5_glu(y_s5, u, s5_d, cast(w_glu), b_glu)

    wb = cast(w_branch)
    merged = _branch_merge(o_attn, o_ssd, o_s5, wb[:Q_W], wb[Q_W:Q_W + SSD_INNER], wb[Q_W + SSD_INNER:], gates)
    return _mm_wres(merged, [(w_out_all, layer)], _ep_plain, n=w_out_all.shape[2], name="out_proj")


def kernel(x, c, positions, w_mod, b_mod, norm_mix_pre, norm_mix_post, norm_ffn_pre, norm_ffn_post, w_in, attn_sinks, conv_w, conv_b, dt_bias, a_log, ssd_d, ssd_norm, s5_lam_re, s5_lam_im, s5_log_step, s5_b_re, s5_b_im, s5_c_re, s5_c_im, s5_d, s5_w_glu, s5_b_glu, w_branch, w_out, ffn_w1, ffn_w3, ffn_w2, w_router, moe_w1, moe_w3, moe_w2):
    bsz, l, d = x.shape
    depth = w_mod.shape[0]
    assert l % SSD_CHUNK == 0 and l % ATTN_BLOCK == 0 and l % S5_CHUNK == 0
    rope = _rope_tables(positions)
    mod = _modulation(c, w_mod, b_mod)
    cast = lambda w: w.astype(BF)

    def mods(i):
        return [mod[i, :, k * d:(k + 1) * d] for k in range(6)]

    sh1, sc1, g1, sh2, sc2, g2 = mods(0)
    h = _pre_norm(x, norm_mix_pre[0], sh1, sc1)
    for i in range(depth):
        m = _hybrid_mixer(h.reshape(bsz * l, d), rope, bsz, l, i, w_in, attn_sinks[i], conv_w[i], conv_b[i],
                          dt_bias[i], a_log[i], ssd_d[i], ssd_norm[i], s5_lam_re[i], s5_lam_im[i],
                          s5_log_step[i], s5_b_re[i], s5_b_im[i], s5_c_re[i], s5_c_im[i], s5_d[i],
                          s5_w_glu[i], s5_b_glu[i], w_branch[i], w_out)
        is_moe = i % 2 == 1
        x, h = _post_norm(m, x, g1, norm_mix_post[i], norm_ffn_pre[i], sh2, sc2, h_dtype=F32 if is_moe else BF)
        h2 = h.reshape(bsz * l, d)
        j = i // 2
        if i % 2 == 0:
            act = _mm_wres(h2, [(ffn_w1, j), (ffn_w3, j)], _ep_swiglu, n=ffn_w1.shape[2], tn=256,
                           name="swiglu_up")
            f = _matmul(act, cast(ffn_w2[j]), tm=512, name="ffn_down")
        else:
            f = _moe(h2, w_router[j], moe_w1[j], moe_w3[j], moe_w2[j])
        if i + 1 < depth:
            sh1, sc1, g1n, sh2n, sc2n, g2n = mods(i + 1)
            x, h = _post_norm(f, x, g2, norm_ffn_post[i], norm_mix_pre[i + 1], sh1, sc1)
            g1, sh2, sc2, g2 = g1n, sh2n, sc2n, g2n
        else:
            x, _ = _post_norm(f, x, g2, norm_ffn_post[i])
    return x
```

```python
import functools
import math

import jax
import jax.numpy as jnp
import numpy as np
from jax import lax
from jax.experimental import pallas as pl
from jax.experimental.pallas import tpu as pltpu

BF = jnp.bfloat16
F32 = jnp.float32

HEAD_DIM = 64
Q_HEADS = 32
KV_HEADS = 4
Q_GROUP = Q_HEADS // KV_HEADS
Q_W = Q_HEADS * HEAD_DIM
KV_W = KV_HEADS * HEAD_DIM
ATTN_BLOCK = 128
ROT_DIM = 16
ROT_HALF = ROT_DIM // 2
ROPE_THETA = 500000.0
SSD_INNER = 2048
SSD_HEADS = 32
SSD_HEAD_DIM = 64
SSD_GROUPS = 4
SSD_STATE = 128
SSD_CONV = 4
SSD_CHUNK = 128
SSD_CONV_CH = SSD_INNER + 2 * SSD_GROUPS * SSD_STATE
S5_WIDTH = 1536
S5_GROUP_SIZE = 16
S5_GROUPS = 96
S5_STATE = 64
S5_CHUNK = 16
S5_LANE_GROUPS = 8
N_EXPERTS = 8
RMS_EPS = 1e-6
LANES = 128
MXU_COLS = 256
MIB = 1024 * 1024
NEG_BIG = -1e30


def _cparams(semantics, vmem_bytes):
    limit = int(min(max(vmem_bytes * 1.25 + 4 * MIB, 16 * MIB), 60 * MIB))
    return pltpu.CompilerParams(dimension_semantics=semantics, vmem_limit_bytes=limit)


def _pick(n, candidates):
    for c in candidates:
        if n % c == 0:
            return c
    raise ValueError(f"no tile in {candidates} divides {n}")


def _silu(x):
    return x * jax.nn.sigmoid(x)


def _mod_kernel(c_ref, w_ref, b_ref, o_ref):
    c = c_ref[...]
    a = _silu(c).astype(BF)
    o_ref[0] = jnp.dot(a, w_ref[0].astype(BF), preferred_element_type=F32) + b_ref[0]


def _modulation(c, w_mod, b_mod):
    depth, d, n = w_mod.shape
    bsz = c.shape[0]
    rows = 8
    c_pad = jnp.zeros((rows, d), F32).at[:bsz].set(c)
    tn = _pick(n, (512, 256, 128))
    out = pl.pallas_call(
        _mod_kernel,
        out_shape=jax.ShapeDtypeStruct((depth, rows, n), F32),
        grid=(depth, n // tn),
        in_specs=[pl.BlockSpec((rows, d), lambda l, j: (0, 0)),
                  pl.BlockSpec((1, d, tn), lambda l, j: (l, 0, j)),
                  pl.BlockSpec((1, 1, tn), lambda l, j: (l, 0, j))],
        out_specs=pl.BlockSpec((1, rows, tn), lambda l, j: (l, 0, j)),
        compiler_params=_cparams(("parallel", "arbitrary"), 2 * d * tn * 4 + d * tn * 2),
        name="adaln_modulation",
    )(c_pad, w_mod, b_mod.reshape(depth, 1, n))
    return out[:, :bsz]


def _rms(x, w):
    return x * lax.rsqrt(jnp.mean(x * x, axis=-1, keepdims=True) + RMS_EPS) * w


def _pre_norm_kernel(x_ref, w_ref, sh_ref, sc_ref, h_ref):
    h = _rms(x_ref[0], w_ref[...])
    h_ref[0] = (h * (1.0 + sc_ref[0]) + sh_ref[0]).astype(h_ref.dtype)


def _pre_norm(x, w, shift, scale):
    bsz, l, d = x.shape
    tl = _pick(l, (256, 128))
    vec = pl.BlockSpec((1, 1, d), lambda b, i: (b, 0, 0))
    return pl.pallas_call(
        _pre_norm_kernel,
        out_shape=jax.ShapeDtypeStruct((bsz, l, d), BF),
        grid=(bsz, l // tl),
        in_specs=[pl.BlockSpec((1, tl, d), lambda b, i: (b, i, 0)),
                  pl.BlockSpec((1, d), lambda b, i: (0, 0)), vec, vec],
        out_specs=pl.BlockSpec((1, tl, d), lambda b, i: (b, i, 0)),
        compiler_params=_cparams(("parallel", "parallel"), 2 * tl * d * 6 + 2 * tl * d * 4),
        name="pre_norm",
    )(x, w.reshape(1, d), shift.reshape(bsz, 1, d), scale.reshape(bsz, 1, d))


def _post_norm_kernel(m_ref, x_ref, g_ref, wpost_ref, wpre_ref, sh_ref, sc_ref, xo_ref, h_ref):
    xn = x_ref[0] + g_ref[0] * _rms(m_ref[0].astype(F32), wpost_ref[...])
    xo_ref[0] = xn
    h = _rms(xn, wpre_ref[...])
    h_ref[0] = (h * (1.0 + sc_ref[0]) + sh_ref[0]).astype(h_ref.dtype)


def _post_norm_last_kernel(m_ref, x_ref, g_ref, wpost_ref, xo_ref):
    xo_ref[0] = x_ref[0] + g_ref[0] * _rms(m_ref[0].astype(F32), wpost_ref[...])


def _post_norm(m, x, gate, w_post, w_pre=None, shift=None, scale=None, h_dtype=BF):
    bsz, l, d = x.shape
    tl = _pick(l, (256, 128))
    row = pl.BlockSpec((1, tl, d), lambda b, i: (b, i, 0))
    vec = pl.BlockSpec((1, 1, d), lambda b, i: (b, 0, 0))
    par = pl.BlockSpec((1, d), lambda b, i: (0, 0))
    m = m.reshape(bsz, l, d)
    gate = gate.reshape(bsz, 1, d)
    cp = _cparams(("parallel", "parallel"), 2 * tl * d * (2 + 4 + 4 + 2) + 3 * tl * d * 4)
    if w_pre is None:
        return pl.pallas_call(
            _post_norm_last_kernel,
            out_shape=jax.ShapeDtypeStruct((bsz, l, d), F32),
            grid=(bsz, l // tl),
            in_specs=[row, row, vec, par],
            out_specs=row,
            compiler_params=cp,
            name="post_norm_last",
        )(m, x, gate, w_post.reshape(1, d)), None
    return pl.pallas_call(
        _post_norm_kernel,
        out_shape=(jax.ShapeDtypeStruct((bsz, l, d), F32), jax.ShapeDtypeStruct((bsz, l, d), h_dtype)),
        grid=(bsz, l // tl),
        in_specs=[row, row, vec, par, par, vec, vec],
        out_specs=(row, row),
        compiler_params=cp,
        name="post_norm",
    )(m, x, gate, w_post.reshape(1, d), w_pre.reshape(1, d), shift.reshape(bsz, 1, d),
      scale.reshape(bsz, 1, d))


def _mm_call(body, a_list, w_list, extras, extra_specs, *, n, tm, tn, out_dtype, name):
    m = a_list[0].shape[0]
    tm = min(tm, m)
    assert m % tm == 0 and n % tn == 0
    a_specs = [pl.BlockSpec((tm, a.shape[1]), lambda i, j: (i, 0)) for a in a_list]
    w_specs = [pl.BlockSpec((w.shape[0], tn), lambda i, j: (0, j)) for w in w_list]
    vmem = sum(2 * tm * a.shape[1] * a.dtype.itemsize for a in a_list)
    vmem += sum(2 * w.shape[0] * tn * w.dtype.itemsize for w in w_list)
    vmem += 2 * tm * tn * 4 * (1 + len(extras)) + 3 * tm * tn * 4
    return pl.pallas_call(
        body,
        out_shape=jax.ShapeDtypeStruct((m, n), out_dtype),
        grid=(m // tm, n // tn),
        in_specs=a_specs + w_specs + list(extra_specs(tm, tn)),
        out_specs=pl.BlockSpec((tm, tn), lambda i, j: (i, j)),
        compiler_params=_cparams(("parallel", "arbitrary"), vmem),
        name=name,
    )(*a_list, *w_list, *extras)


def _dot(a, w):
    return jnp.dot(a, w, preferred_element_type=F32)


def _plain_body(a_ref, w_ref, o_ref):
    o_ref[...] = _dot(a_ref[...], w_ref[...]).astype(o_ref.dtype)


def _matmul(a, w, *, tn=None, tm=1024, out_dtype=BF, body=_plain_body, name="matmul"):
    n = w.shape[1]
    tn = tn or _pick(n, (512, 256, 128))
    return _mm_call(body, [a], [w], [], lambda tm_, tn_: [], n=n, tm=tm, tn=tn,
                    out_dtype=out_dtype, name=name)


def _wres_kernel(n_w, shift, epilogue, a_ref, *refs):
    n_in = n_w * (2 if shift else 1)
    w_refs, nxt_refs = refs[:n_w], refs[n_w:n_in]
    extras, o_ref, wb = refs[n_in:-n_w - 1], refs[-n_w - 1], refs[-n_w:]

    @pl.when(pl.program_id(1) == 0)
    def _():
        for idx, (w_ref, b) in enumerate(zip(w_refs, wb)):
            w = w_ref[0]
            if shift:
                w = jnp.concatenate([w[:, shift:], nxt_refs[idx][0][:, :shift]], axis=1)
            b[...] = w.astype(BF)

    a = a_ref[...]
    tn = o_ref.shape[1]
    sub = MXU_COLS if tn % MXU_COLS == 0 else tn
    for c in range(0, tn, sub):
        epilogue([_dot(a, b[:, c:c + sub]) for b in wb], extras, o_ref.at[:, c:c + sub], c, tn)


def _mm_wres(a, weights, epilogue, *, n, col0=0, tn=None, extras=(), extra_specs=(), tm=1024,
             out_dtype=BF, name):
    m, k = a.shape
    tm = min(tm, m)
    shift = col0 % LANES
    base = col0 - shift
    tn = tn or next(t for t in (512, 256, 128) if n % t == 0 and base % t == 0)
    assert m % tm == 0 and n % tn == 0 and base % tn == 0
    c0, per = base // tn, tn // LANES
    w_specs = [pl.BlockSpec((1, k, tn), functools.partial(lambda lead, j, i: (lead, 0, c0 + j), lead))
               for _, lead in weights]
    if shift:
        w_specs += [pl.BlockSpec((1, k, LANES),
                                 functools.partial(lambda lead, j, i: (lead, 0, (c0 + j + 1) * per), lead))
                    for _, lead in weights]
    w_args = [w for w, _ in weights] * (2 if shift else 1)
    vmem = 2 * tm * k * 2 + len(weights) * (2 * k * (tn + LANES) * 4 + 2 * k * tn * 2) + 2 * tm * tn * 4
    vmem += 4 * tm * tn * 4
    return pl.pallas_call(
        functools.partial(_wres_kernel, len(weights), shift, epilogue),
        out_shape=jax.ShapeDtypeStruct((m, n), out_dtype),
        grid=(n // tn, m // tm),
        in_specs=[pl.BlockSpec((tm, k), lambda j, i: (i, 0))] + w_specs + list(extra_specs),
        out_specs=pl.BlockSpec((tm, tn), lambda j, i: (i, j)),
        scratch_shapes=[pltpu.VMEM((k, tn), BF)] * len(weights),
        compiler_params=_cparams(("arbitrary", "arbitrary"), vmem),
        name=name,
    )(a, *w_args, *extras)


def _ep_plain(accs, extras, o_ref, col, tile_w):
    o_ref[...] = accs[0].astype(o_ref.dtype)


def _ep_sigmoid(accs, extras, o_ref, col, tile_w):
    o_ref[...] = jax.nn.sigmoid(accs[0]).astype(o_ref.dtype)


def _ep_swiglu(accs, extras, o_ref, col, tile_w):
    o_ref[...] = (_silu(accs[0]) * accs[1]).astype(o_ref.dtype)


def _ep_qkv(accs, extras, o_ref, col, tile_w):
    cos_ref, sa_ref, sb_ref = extras
    acc = accs[0]
    first = pl.program_id(0) * tile_w + col
    cols = []
    for c in range(acc.shape[1] // LANES):
        r = acc[:, c * LANES:(c + 1) * LANES]
        g0 = first + c * LANES
        roped = (r * cos_ref[...] + pltpu.roll(r, ROT_HALF, axis=1) * sa_ref[...]
                 + pltpu.roll(r, LANES - ROT_HALF, axis=1) * sb_ref[...])
        r = jnp.where(g0 < Q_W + KV_W, roped, r)
        cols.append(r * jnp.where(g0 < Q_W, HEAD_DIM ** -0.5, 1.0))
    o_ref[...] = jnp.concatenate(cols, axis=1).astype(o_ref.dtype)


def _merge_body(oa_ref, ob_ref, oc_ref, wa_ref, wb_ref, wc_ref, ga_ref, gb_ref, gc_ref, o_ref):
    acc = ga_ref[...].astype(F32) * _dot(oa_ref[...], wa_ref[...])
    acc += gb_ref[...].astype(F32) * _dot(ob_ref[...], wb_ref[...])
    acc += gc_ref[...].astype(F32) * _dot(oc_ref[...], wc_ref[...])
    o_ref[...] = acc.astype(o_ref.dtype)


def _branch_merge(o_attn, o_ssd, o_s5, wa, wb, wc, gates):
    d = wa.shape[1]
    tn = _pick(d, (512, 256, 128))
    nt = d // tn

    def gate_specs(tm_, tn_):
        return [pl.BlockSpec((tm_, tn_), functools.partial(lambda br, i, j: (i, br * nt + j), br))
                for br in range(3)]

    return _mm_call(_merge_body, [o_attn, o_ssd, o_s5], [wa, wb, wc], [gates, gates, gates],
                    gate_specs, n=d, tm=512, tn=tn, out_dtype=BF, name="branch_merge")


def _attn_kernel(sink_ref, q_ref, kc_ref, kp_ref, vc_ref, vp_ref, o_ref):
    n = pl.program_id(1)
    qi = lax.broadcasted_iota(jnp.int32, (ATTN_BLOCK, 2 * ATTN_BLOCK), 0)
    sj = lax.broadcasted_iota(jnp.int32, (ATTN_BLOCK, 2 * ATTN_BLOCK), 1)
    rel = qi + ATTN_BLOCK - sj
    valid = (rel >= 0) & (rel < ATTN_BLOCK) & ((sj >= ATTN_BLOCK) | (n > 0))
    q = q_ref[0]
    kv = 2 * ATTN_BLOCK
    pair_w = 2 * HEAD_DIM
    lane = lax.broadcasted_iota(jnp.int32, (ATTN_BLOCK, pair_w), 1)
    first = lane < HEAD_DIM
    zeros = jnp.zeros((kv, HEAD_DIM), BF)
    ones = jnp.ones((kv, HEAD_DIM), BF)
    outs = []
    for g in range(KV_HEADS):
        sl = slice(g * HEAD_DIM, (g + 1) * HEAD_DIM)
        kbt = jnp.concatenate([kp_ref[0, g], kc_ref[0, g]], axis=1)
        vb = jnp.concatenate([vp_ref[0][:, sl], vc_ref[0][:, sl]], axis=0)
        pv = jnp.concatenate([jnp.concatenate([vb, zeros, ones, zeros], axis=1),
                              jnp.concatenate([zeros, vb, zeros, ones], axis=1)], axis=0)
        for r in range(0, Q_GROUP, 2):
            ps, sinks = [], []
            for k in range(2):
                hd = g * Q_GROUP + r + k
                s = _dot(q[:, hd * HEAD_DIM:(hd + 1) * HEAD_DIM], kbt)
                s = jnp.where(valid, s, NEG_BIG)
                mx = jnp.maximum(jnp.max(s, axis=-1, keepdims=True), sink_ref[hd])
                ps.append(jnp.exp(s - mx).astype(BF))
                sinks.append(jnp.exp(sink_ref[hd] - mx))
            res = _dot(jnp.concatenate(ps, axis=1), pv)
            den = res[:, pair_w:] + jnp.where(first, sinks[0], sinks[1])
            outs.append(res[:, :pair_w] / den)
    o_ref[0] = jnp.concatenate(outs, axis=1).astype(o_ref.dtype)


def _attention(qkv, sinks, bsz, l):
    nb = l // ATTN_BLOCK
    qkv = qkv.reshape(bsz, l, Q_W + 2 * KV_W)
    k_t = qkv[:, :, Q_W:Q_W + KV_W].reshape(bsz, l, KV_HEADS, HEAD_DIM).transpose(0, 2, 3, 1)
    vcol = Q_W // KV_W + 1
    kspec = lambda idx: pl.BlockSpec((1, KV_HEADS, HEAD_DIM, ATTN_BLOCK), lambda b, n: (b, 0, 0, idx(n)))
    vspec = lambda idx: pl.BlockSpec((1, ATTN_BLOCK, KV_W), lambda b, n: (b, idx(n), vcol))
    cur = lambda n: n
    prev = lambda n: jnp.maximum(n - 1, 0)
    out = pl.pallas_call(
        _attn_kernel,
        out_shape=jax.ShapeDtypeStruct((bsz, l, Q_W), BF),
        grid=(bsz, nb),
        in_specs=[pl.BlockSpec(memory_space=pltpu.SMEM),
                  pl.BlockSpec((1, ATTN_BLOCK, Q_W), lambda b, n: (b, n, 0)),
                  kspec(cur), kspec(prev), vspec(cur), vspec(prev)],
        out_specs=pl.BlockSpec((1, ATTN_BLOCK, Q_W), lambda b, n: (b, n, 0)),
        compiler_params=_cparams(("parallel", "parallel"), 8 * MIB),
        name="sliding_window_attention",
    )(sinks.astype(F32), qkv, k_t, k_t, qkv, qkv)
    return out.reshape(bsz * l, Q_W)


def _split_dot(f, e):
    hi = f.astype(BF)
    lo = (f - hi.astype(F32)).astype(BF)
    return _dot(hi, e) + _dot(lo, e)


def _ssd_kernel(z_ref, xbc_ref, dt_ref, cw_ref, cb_ref, dtb_ref, alog_ref, dsk_ref, nw_ref,
                o_ref, ext_ref, state_ref):
    c = pl.program_id(1)
    q = SSD_CHUNK
    halo = 8

    @pl.when(c == 0)
    def _():
        ext_ref[0:halo, :] = jnp.zeros((halo, SSD_CONV_CH), F32)
        state_ref[...] = jnp.zeros_like(state_ref)

    ext_ref[halo:halo + q, :] = xbc_ref[0].astype(F32)
    conv = cb_ref[...] + cw_ref[SSD_CONV - 1:SSD_CONV, :] * ext_ref[halo:halo + q, :]
    for j in range(SSD_CONV - 1):
        off = halo - (SSD_CONV - 1) + j
        conv = conv + cw_ref[j:j + 1, :] * ext_ref[off:off + q, :]
    tail = ext_ref[q:q + halo, :]
    ext_ref[0:halo, :] = tail
    xbc = _silu(conv)
    xs = xbc[:, :SSD_INNER]
    xs_bf = xs.astype(BF)
    bm = xbc[:, SSD_INNER:SSD_INNER + SSD_GROUPS * SSD_STATE].astype(BF)
    cm = xbc[:, SSD_INNER + SSD_GROUPS * SSD_STATE:].astype(BF)

    dt = jax.nn.softplus(dt_ref[0] + dtb_ref[...])
    a = -jnp.exp(alog_ref[...])
    cs = dt * a
    row = lax.broadcasted_iota(jnp.int32, (q, LANES), 0)
    sh = 1
    while sh < q:
        cs = cs + jnp.where(row >= sh, pltpu.roll(cs, sh, axis=0), 0.0)
        sh *= 2
    cs_last = cs[q - 1:q, :]
    ecs = jnp.exp(cs)
    dte = jnp.exp(cs_last - cs) * dt
    cs_t = cs.T
    dt_t = dt.T

    hl = lax.broadcasted_iota(jnp.int32, (LANES, SSD_INNER), 0)
    hc = lax.broadcasted_iota(jnp.int32, (LANES, SSD_INNER), 1)
    expand = jnp.where(hl == hc // SSD_HEAD_DIM, 1.0, 0.0).astype(BF)
    ecs_x = _split_dot(ecs, expand)
    dte_x = _split_dot(dte, expand)

    li = lax.broadcasted_iota(jnp.int32, (q, q), 0)
    si = lax.broadcasted_iota(jnp.int32, (q, q), 1)
    causal = li >= si
    heads_per_group = SSD_HEADS // SSD_GROUPS
    gw = heads_per_group * SSD_HEAD_DIM
    y_parts = []
    for g in range(SSD_GROUPS):
        ns = slice(g * SSD_STATE, (g + 1) * SSD_STATE)
        gs = slice(g * gw, (g + 1) * gw)
        bm_g, cm_g = bm[:, ns], cm[:, ns]
        cb = lax.dot_general(cm_g, bm_g, (((1,), (1,)), ((), ())), preferred_element_type=F32)
        for r in range(heads_per_group):
            hd = g * heads_per_group + r
            seg = cs[:, hd:hd + 1] - cs_t[hd:hd + 1, :]
            decay = jnp.exp(jnp.where(causal, seg, NEG_BIG))
            w = (cb * decay * dt_t[hd:hd + 1, :]).astype(BF)
            y_parts.append(_dot(w, xs_bf[:, hd * SSD_HEAD_DIM:(hd + 1) * SSD_HEAD_DIM]))
    y = jnp.concatenate(y_parts, axis=1)
    xw = (xs * dte_x).astype(BF)
    for g in range(SSD_GROUPS):
        ns = slice(g * SSD_STATE, (g + 1) * SSD_STATE)
        gs = slice(g * gw, (g + 1) * gw)
        prev = state_ref[:, gs]
        y_off = _dot(cm[:, ns], prev.astype(BF)) * ecs_x[:, gs]
        upd = lax.dot_general(bm[:, ns], xw[:, gs], (((0,), (0,)), ((), ())),
                              preferred_element_type=F32)
        state_ref[:, gs] = prev * ecs_x[q - 1:q, gs] + upd
        yg = y[:, gs] + y_off + dsk_ref[:, gs] * xs[:, gs]
        yg = yg * _silu(z_ref[0, :, gs].astype(F32))
        yg = yg * lax.rsqrt(jnp.mean(yg * yg, axis=-1, keepdims=True) + RMS_EPS)
        o_ref[0, :, gs] = (yg * nw_ref[:, gs]).astype(o_ref.dtype)


def _ssd(z, xbc, dt_raw, conv_w, conv_b, dt_bias, a_log, d_skip, norm_w, bsz, l):
    q = SSD_CHUNK
    pad = LANES - SSD_HEADS
    row = lambda w: pl.BlockSpec((1, q, w), lambda b, c: (b, c, 0))
    par = lambda r, w: pl.BlockSpec((r, w), lambda b, c: (0, 0))
    out = pl.pallas_call(
        _ssd_kernel,
        out_shape=jax.ShapeDtypeStruct((bsz, l, SSD_INNER), BF),
        grid=(bsz, l // q),
        in_specs=[row(SSD_INNER), row(SSD_CONV_CH), row(LANES),
                  par(SSD_CONV, SSD_CONV_CH), par(1, SSD_CONV_CH), par(1, LANES), par(1, LANES),
                  par(1, SSD_INNER), par(1, SSD_INNER)],
        out_specs=row(SSD_INNER),
        scratch_shapes=[pltpu.VMEM((q + 8, SSD_CONV_CH), F32), pltpu.VMEM((SSD_STATE, SSD_INNER), F32)],
        compiler_params=_cparams(("parallel", "arbitrary"), 24 * MIB),
        name="ssd_chunked",
    )(z.reshape(bsz, l, SSD_INNER), xbc.reshape(bsz, l, SSD_CONV_CH), dt_raw.reshape(bsz, l, LANES),
      conv_w, conv_b.reshape(1, -1), jnp.pad(dt_bias, (0, pad)).reshape(1, LANES),
      jnp.pad(a_log, (0, pad)).reshape(1, LANES),
      jnp.repeat(d_skip, SSD_HEAD_DIM).reshape(1, SSD_INNER), norm_w.reshape(1, SSD_INNER))
    return out.reshape(bsz * l, SSD_INNER)


def _s5_operators(lam_re, lam_im, log_step, b_re, b_im, c_re, c_im, n_chunks):
    hp = lax.Precision.HIGHEST
    qn = S5_CHUNK
    lr, li = lam_re.astype(F32), lam_im.astype(F32)
    step = jnp.exp(log_step.astype(F32))[:, None]
    mag = jnp.exp(lr * step)
    a_re, a_im = mag * jnp.cos(li * step), mag * jnp.sin(li * step)
    den = lr * lr + li * li
    coef_re = ((a_re - 1.0) * lr + a_im * li) / den
    coef_im = (a_im * lr - (a_re - 1.0) * li) / den
    br, bi = b_re.astype(F32), b_im.astype(F32)
    bb_re = coef_re[..., None] * br - coef_im[..., None] * bi
    bb_im = coef_re[..., None] * bi + coef_im[..., None] * br

    def power(d):
        d = d.astype(F32)[None, :, None]
        m = jnp.exp(lr[:, None, :] * step[:, None, :] * d)
        ang = li[:, None, :] * step[:, None, :] * d
        return m * jnp.cos(ang), m * jnp.sin(ang)

    p_re, p_im = power(jnp.arange(qn + 1))
    cr, ci = c_re.astype(F32), c_im.astype(F32)
    m_re = cr[:, None] * p_re[:, :, None, :] - ci[:, None] * p_im[:, :, None, :]
    m_im = cr[:, None] * p_im[:, :, None, :] + ci[:, None] * p_re[:, :, None, :]
    kern = (jnp.einsum('gdip,gpj->gdij', m_re, bb_re, precision=hp)
            - jnp.einsum('gdip,gpj->gdij', m_im, bb_im, precision=hp))
    gl, nlg, gs, ps = S5_LANE_GROUPS, S5_GROUPS // S5_LANE_GROUPS, S5_GROUP_SIZE, S5_STATE
    lag_op = kern[:, :qn].reshape(nlg, gl, qn, gs, gs).transpose(0, 1, 4, 2, 3)
    lag_op = lag_op.reshape(nlg, LANES, qn * gs)
    e_re, e_im = p_re[:, qn - 1 - jnp.arange(qn)], p_im[:, qn - 1 - jnp.arange(qn)]
    bt_re, bt_im = bb_re.transpose(0, 2, 1)[:, None], bb_im.transpose(0, 2, 1)[:, None]
    be = jnp.stack([e_re[:, :, None, :] * bt_re - e_im[:, :, None, :] * bt_im,
                    e_re[:, :, None, :] * bt_im + e_im[:, :, None, :] * bt_re], axis=3)
    b_end = be.reshape(nlg, gl, qn, gs, 2 * ps).transpose(0, 2, 1, 3, 4)
    b_end = b_end.reshape(nlg, qn * LANES, 2 * ps)
    ci = jnp.stack([m_re[:, 1:].transpose(0, 3, 1, 2), -m_im[:, 1:].transpose(0, 3, 1, 2)], axis=1)
    c_in = ci.reshape(nlg, gl, 2, ps, qn * gs).transpose(0, 2, 1, 3, 4)
    c_in = c_in.reshape(nlg, 2 * gl * ps, qn * gs)
    n_steps = max(1, (n_chunks - 1).bit_length())
    s_re, s_im = power(qn * (2 ** jnp.arange(n_steps)))
    lanes = lambda t: t.reshape(nlg, gl, n_steps, ps).transpose(0, 2, 1, 3).reshape(nlg, n_steps, gl * ps)
    pw1 = jnp.concatenate([lanes(s_re), lanes(s_re)], axis=-1)
    pw2 = jnp.concatenate([-lanes(s_im), lanes(s_im)], axis=-1)
    return lag_op.astype(BF), b_end.astype(BF), c_in.astype(BF), pw1, pw2


def _s5_kernel(u_ref, lag_ref, bend_ref, cin_ref, t_out_ref, t_state_ref, pw1_ref, pw2_ref, y_ref,
               toep_ref, bend_full, cin_full, ucat_ref):
    qn, gs, ps = S5_CHUNK, S5_GROUP_SIZE, S5_STATE
    n_chunks = u_ref.shape[1] // qn
    n_steps = pw1_ref.shape[1]
    st = pw1_ref.shape[2]
    half = st // 2

    @pl.when(pl.program_id(1) == 0)
    def _():
        blk = lambda d: slice(d * LANES, (d + 1) * LANES)
        col = lambda n: lax.broadcasted_iota(jnp.int32, (n, 1), 0)
        out_group = lax.broadcasted_iota(jnp.int32, (1, LANES), 1) // gs
        lag_diag = col(LANES) // gs == out_group
        cin_diag = (col(st) % half) // ps == out_group
        toep_ref[...] = jnp.zeros_like(toep_ref)
        for d in range(qn):
            lag_d = jnp.where(lag_diag, _dot(lag_ref[0], t_out_ref[:, blk(d)]), 0.0).astype(BF)
            for s in range(qn - d):
                toep_ref[blk(s), blk(s + d)] = lag_d
            cin_full[:, blk(d)] = jnp.where(cin_diag, _dot(cin_ref[0], t_out_ref[:, blk(d)]), 0.0).astype(BF)
        in_group = (col(qn * LANES) % LANES) // gs
        for k in range(st // LANES):
            state_group = ((k * LANES + lax.broadcasted_iota(jnp.int32, (1, LANES), 1)) % half) // ps
            bend_full[:, blk(k)] = jnp.where(in_group == state_group,
                                             _dot(bend_ref[0], t_state_ref[:, blk(k)]), 0.0).astype(BF)

    for s in range(qn):
        ucat_ref[:, s * LANES:(s + 1) * LANES] = u_ref[0, pl.ds(s, n_chunks, stride=qn), :].astype(BF)
    ucat = ucat_ref[...]
    x = _dot(ucat, bend_full[...])
    row = lax.broadcasted_iota(jnp.int32, x.shape, 0)
    for k in range(n_steps):
        sh = 1 << k
        xs = jnp.where(row >= sh, pltpu.roll(x, sh, axis=0), 0.0)
        x = x + pw1_ref[0, k:k + 1, :] * xs + pw2_ref[0, k:k + 1, :] * pltpu.roll(xs, half, axis=1)
    x_in = jnp.where(row >= 1, pltpu.roll(x, 1, axis=0), 0.0)
    y = _dot(ucat, toep_ref[...]) + _dot(x_in.astype(BF), cin_full[...])
    for t in range(qn):
        y_ref[0, pl.ds(t, n_chunks, stride=qn), :] = y[:, t * LANES:(t + 1) * LANES]


def _s5_scan(u, ops, bsz, l):
    lag_op, b_end, c_in, pw1, pw2 = ops
    qn, gs, ps = S5_CHUNK, S5_GROUP_SIZE, S5_STATE
    nlg = S5_WIDTH // LANES
    nc = l // qn
    wide = qn * LANES
    st = pw1.shape[2]
    r, c = np.arange(qn * gs)[:, None], np.arange(wide)[None, :]
    t_out = jnp.asarray((r // gs == c // LANES) & (r % gs == c % gs), BF)
    r, c = np.arange(2 * ps)[:, None], np.arange(st)[None, :]
    t_state = jnp.asarray((r // ps == c // (st // 2)) & (r % ps == c % ps), BF)
    wspec = lambda rows, cols: pl.BlockSpec((1, rows, cols), lambda g, b: (g, 0, 0))
    full = lambda a: pl.BlockSpec(a.shape, lambda g, b: (0, 0))
    seq = pl.BlockSpec((1, l, LANES), lambda g, b: (b, 0, g))
    y = pl.pallas_call(
        _s5_kernel,
        out_shape=jax.ShapeDtypeStruct((bsz, l, S5_WIDTH), F32),
        grid=(nlg, bsz),
        in_specs=[seq, wspec(LANES, qn * gs), wspec(wide, 2 * ps), wspec(st, qn * gs), full(t_out), full(t_state),
                  wspec(pw1.shape[1], st), wspec(pw1.shape[1], st)],
        out_specs=seq,
        scratch_shapes=[pltpu.VMEM((wide, wide), BF), pltpu.VMEM((wide, st), BF), pltpu.VMEM((st, wide), BF),
                        pltpu.VMEM((nc, wide), BF)],
        compiler_params=_cparams(("arbitrary", "arbitrary"),
                                 4 * l * LANES * 4 + wide * wide * 2 + 2 * wide * st * 2 + 4 * wide * LANES * 2
                                 + nc * wide * 2 + 3 * nc * wide * 4 + 4 * nc * st * 4),
        name="s5_chunked_scan",
    )(u, lag_op, b_end, c_in, t_out, t_state, pw1, pw2)
    return y.reshape(bsz * l, S5_WIDTH)


def _s5_glu_kernel(y_ref, u_ref, d_ref, w_ref, b_ref, o_ref):
    y = y_ref[...].astype(F32) + d_ref[...] * u_ref[...].astype(F32)
    v = jax.nn.gelu(y, approximate=True)
    o_ref[...] = (v * jax.nn.sigmoid(_dot(v.astype(BF), w_ref[...]) + b_ref[...])).astype(o_ref.dtype)


def _s5_glu(y, u, d_skip, w_glu, b_glu):
    m, w = y.shape
    tm = _pick(m, (512, 256, 128))
    row = pl.BlockSpec((tm, w), lambda i: (i, 0))
    par = pl.BlockSpec((1, w), lambda i: (0, 0))
    return pl.pallas_call(
        _s5_glu_kernel,
        out_shape=jax.ShapeDtypeStruct((m, w), BF),
        grid=(m // tm,),
        in_specs=[row, row, par, pl.BlockSpec((w, w), lambda i: (0, 0)), par],
        out_specs=row,
        compiler_params=_cparams(("parallel",), 4 * tm * w * 4 + 2 * tm * w * 2 + 2 * w * w * 2 + 4 * tm * w * 4),
        name="s5_gelu_glu",
    )(y, u, d_skip.reshape(1, w), w_glu, b_glu.reshape(1, w))


MOE_ROW_TILE = 256
INFO_E1, INFO_E2, INFO_W1, INFO_W2, INFO_R1, INFO_R2 = range(6)


def _split_bf16(x):
    hi = x.astype(BF)
    return hi, (x - hi.astype(F32)).astype(BF)


def _router_kernel(h_ref, w_ref, info_ref, cnt_ref, carry_ref):
    i = pl.program_id(0)

    @pl.when(i == 0)
    def _():
        carry_ref[...] = jnp.zeros_like(carry_ref)

    hh, hl = _split_bf16(h_ref[...])
    wh, wl = _split_bf16(w_ref[...])
    logits = _dot(hh, wh) + _dot(hl, wh) + _dot(hh, wl)
    tm = logits.shape[0]
    lane = lax.broadcasted_iota(jnp.int32, logits.shape, 1)
    lg = jnp.where(lane < N_EXPERTS, logits, -jnp.inf)
    m1 = jnp.max(lg, axis=-1, keepdims=True)
    i1 = jnp.min(jnp.where(lg == m1, lane, LANES), axis=-1, keepdims=True)
    lg2 = jnp.where(lane == i1, -jnp.inf, lg)
    m2 = jnp.max(lg2, axis=-1, keepdims=True)
    i2 = jnp.min(jnp.where(lg2 == m2, lane, LANES), axis=-1, keepdims=True)
    e = jnp.exp(m2 - m1)
    w1 = 1.0 / (1.0 + e)
    w2 = e * w1
    onehot = jnp.where(lane == i1, 1.0, 0.0) + jnp.where(lane == i2, 1.0, 0.0)
    rt = lax.broadcasted_iota(jnp.int32, (tm, tm), 0)
    ct = lax.broadcasted_iota(jnp.int32, (tm, tm), 1)
    before = jnp.where(rt > ct, 1.0, 0.0).astype(BF)
    rank = _dot(before, onehot.astype(BF)) + carry_ref[...]
    r1 = jnp.sum(jnp.where(lane == i1, rank, 0.0), axis=-1, keepdims=True)
    r2 = jnp.sum(jnp.where(lane == i2, rank, 0.0), axis=-1, keepdims=True)
    carry_ref[...] += jnp.sum(onehot, axis=0, keepdims=True)
    cnt_ref[...] = carry_ref[...]
    info = jnp.zeros_like(logits)
    for slot, val in ((INFO_E1, i1.astype(F32)), (INFO_E2, i2.astype(F32)), (INFO_W1, w1), (INFO_W2, w2),
                      (INFO_R1, r1), (INFO_R2, r2)):
        info = jnp.where(lane == slot, val, info)
    info_ref[...] = info


def _router(h, w_router):
    m, d = h.shape
    tm = _pick(m, (512, 256, 128))
    w_pad = jnp.zeros((d, LANES), F32).at[:, :N_EXPERTS].set(w_router.astype(F32))
    return pl.pallas_call(
        _router_kernel,
        out_shape=(jax.ShapeDtypeStruct((m, LANES), F32), jax.ShapeDtypeStruct((1, LANES), F32)),
        grid=(m // tm,),
        in_specs=[pl.BlockSpec((tm, d), lambda i: (i, 0)), pl.BlockSpec((d, LANES), lambda i: (0, 0))],
        out_specs=(pl.BlockSpec((tm, LANES), lambda i: (i, 0)), pl.BlockSpec((1, LANES), lambda i: (0, 0))),
        scratch_shapes=[pltpu.VMEM((1, LANES), F32)],
        compiler_params=_cparams(("arbitrary",), 2 * tm * d * 4 + 2 * tm * d * 2 + 2 * d * LANES * 4
                                 + 2 * tm * tm * 2 + 8 * tm * LANES * 4),
        name="moe_router",
    )(h, w_pad)


def _row_copy(src_hbm, src_row, buf, slot, dst_row, sem):
    return pltpu.make_async_copy(src_hbm.at[pl.ds(src_row, 1)], buf.at[slot, pl.ds(dst_row, 1)], sem.at[slot])


def _dispatch_kernel(tok_ref, nu_ref, h_hbm, o_ref, buf, sem):
    i = pl.program_id(0)
    rows = o_ref.shape[0]
    nu = nu_ref[0]

    def issue(tile, slot):
        def body(r, carry):
            _row_copy(h_hbm, tok_ref[tile * rows + r], buf, slot, r, sem).start()
            return carry
        lax.fori_loop(0, rows, body, 0, unroll=8)

    def wait(slot):
        def body(r, carry):
            _row_copy(h_hbm, 0, buf, slot, 0, sem).wait()
            return carry
        lax.fori_loop(0, rows, body, 0, unroll=8)

    @pl.when(i == 0)
    def _():
        issue(0, 0)

    @pl.when(i + 1 < nu)
    def _():
        issue(i + 1, (i + 1) % 2)

    @pl.when(i < nu)
    def _():
        wait(i % 2)
        o_ref[...] = buf[i % 2].astype(o_ref.dtype)

    @pl.when(i >= nu)
    def _():
        o_ref[...] = jnp.zeros_like(o_ref)


def _dispatch(h, tok_of_row, n_used, n_rows):
    d = h.shape[1]
    rt = MOE_ROW_TILE
    return pl.pallas_call(
        _dispatch_kernel,
        out_shape=jax.ShapeDtypeStruct((n_rows, d), BF),
        grid_spec=pltpu.PrefetchScalarGridSpec(
            num_scalar_prefetch=2,
            grid=(n_rows // rt,),
            in_specs=[pl.BlockSpec(memory_space=pl.ANY)],
            out_specs=pl.BlockSpec((rt, d), lambda i, tok, nu: (i, 0)),
            scratch_shapes=[pltpu.VMEM((2, rt, d), F32), pltpu.SemaphoreType.DMA((2,))]),
        compiler_params=_cparams(("arbitrary",), 2 * rt * d * 4 + 2 * rt * d * 2 + rt * d * 4),
        name="moe_dispatch",
    )(tok_of_row, n_used, h)


def _combine_kernel(p1_ref, p2_ref, y_hbm, o_ref, buf, sem):
    i = pl.program_id(0)
    n = pl.num_programs(0)
    rows = o_ref.shape[0]

    def issue(tile, slot):
        def body(r, carry):
            t = tile * rows + r
            _row_copy(y_hbm, p1_ref[t], buf, slot, r, sem).start()
            _row_copy(y_hbm, p2_ref[t], buf, slot, rows + r, sem).start()
            return carry
        lax.fori_loop(0, rows, body, 0, unroll=4)

    def wait(slot):
        def body(r, carry):
            _row_copy(y_hbm, 0, buf, slot, 0, sem).wait()
            return carry
        lax.fori_loop(0, 2 * rows, body, 0, unroll=8)

    @pl.when(i == 0)
    def _():
        issue(0, 0)

    @pl.when(i + 1 < n)
    def _():
        issue(i + 1, (i + 1) % 2)

    wait(i % 2)
    both = buf[i % 2]
    o_ref[...] = (both[:rows] + both[rows:]).astype(o_ref.dtype)


def _combine(y_rows, pos1, pos2, n_tokens):
    d = y_rows.shape[1]
    tt = 128
    return pl.pallas_call(
        _combine_kernel,
        out_shape=jax.ShapeDtypeStruct((n_tokens, d), BF),
        grid_spec=pltpu.PrefetchScalarGridSpec(
            num_scalar_prefetch=2,
            grid=(n_tokens // tt,),
            in_specs=[pl.BlockSpec(memory_space=pl.ANY)],
            out_specs=pl.BlockSpec((tt, d), lambda i, p1, p2: (i, 0)),
            scratch_shapes=[pltpu.VMEM((2, 2 * tt, d), F32), pltpu.SemaphoreType.DMA((2,))]),
        compiler_params=_cparams(("arbitrary",), 4 * tt * d * 4 + 2 * tt * d * 2 + 2 * tt * d * 4),
        name="moe_combine",
    )(pos1, pos2, y_rows)


def _expert_changed(te_ref, i):
    return jnp.logical_or(i == 0, te_ref[i] != te_ref[jnp.maximum(i - 1, 0)])


def _moe_up_kernel(te_ref, nu_ref, a_ref, w1_ref, w3_ref, rw_ref, o_ref, w1b, w3b):
    i = pl.program_id(1)
    valid = i < nu_ref[0]

    @pl.when(jnp.logical_and(valid, _expert_changed(te_ref, i)))
    def _():
        w1b[...] = w1_ref[0].astype(BF)
        w3b[...] = w3_ref[0].astype(BF)

    @pl.when(valid)
    def _():
        a = a_ref[...]
        tn = o_ref.shape[1]
        sub = MXU_COLS if tn % MXU_COLS == 0 else tn
        for c in range(0, tn, sub):
            act = _silu(_dot(a, w1b[:, c:c + sub])) * _dot(a, w3b[:, c:c + sub])
            o_ref[:, c:c + sub] = (act * rw_ref[...]).astype(o_ref.dtype)

    @pl.when(jnp.logical_not(valid))
    def _():
        o_ref[...] = jnp.zeros_like(o_ref)


def _moe_down_kernel(te_ref, nu_ref, a_ref, w_ref, o_ref, wb):
    i = pl.program_id(1)
    valid = i < nu_ref[0]

    @pl.when(jnp.logical_and(valid, _expert_changed(te_ref, i)))
    def _():
        wb[...] = w_ref[0].astype(BF)

    @pl.when(valid)
    def _():
        o_ref[...] = _dot(a_ref[...], wb[...]).astype(o_ref.dtype)

    @pl.when(jnp.logical_not(valid))
    def _():
        o_ref[...] = jnp.zeros_like(o_ref)


def _grouped_call(kern, rows_in, weights, extras, *, tn, out_dtype, tile_expert, n_used, name):
    n_rows, k = rows_in.shape
    n = weights[0].shape[2]
    rt = MOE_ROW_TILE
    last = lambda i, nu: jnp.minimum(i, nu[0] - 1)
    row_spec = lambda w: pl.BlockSpec((rt, w), lambda j, i, te, nu: (last(i, nu), 0))
    w_spec = pl.BlockSpec((1, k, tn), lambda j, i, te, nu: (te[last(i, nu)], 0, j))
    vmem = 2 * rt * k * 2 + len(weights) * (2 * k * tn * 4 + k * tn * 2) + 2 * rt * tn * 4 + 4 * rt * tn * 4
    return pl.pallas_call(
        kern,
        out_shape=jax.ShapeDtypeStruct((n_rows, n), out_dtype),
        grid_spec=pltpu.PrefetchScalarGridSpec(
            num_scalar_prefetch=2,
            grid=(n // tn, n_rows // rt),
            in_specs=[row_spec(k)] + [w_spec] * len(weights) + [row_spec(e.shape[1]) for e in extras],
            out_specs=pl.BlockSpec((rt, tn), lambda j, i, te, nu: (i, j)),
            scratch_shapes=[pltpu.VMEM((k, tn), BF)] * len(weights)),
        compiler_params=_cparams(("arbitrary", "arbitrary"), vmem),
        name=name,
    )(tile_expert, n_used, rows_in, *weights, *extras)


def _moe(h, w_router, w1, w3, w2):
    m, d = h.shape
    ne, _, f = w1.shape
    rt = MOE_ROW_TILE
    n_rows = 2 * m + ne * rt
    info, counts = _router(h, w_router)
    e1, e2 = info[:, INFO_E1].astype(jnp.int32), info[:, INFO_E2].astype(jnp.int32)
    r1, r2 = info[:, INFO_R1].astype(jnp.int32), info[:, INFO_R2].astype(jnp.int32)
    tiles = (counts[0, :ne].astype(jnp.int32) + rt - 1) // rt
    cum_tiles = jnp.cumsum(tiles)
    base = (cum_tiles - tiles) * rt
    pos1, pos2 = base[e1] + r1, base[e2] + r2
    tok = jnp.arange(m, dtype=F32)
    assert m < 2 ** 24
    per_token = jnp.concatenate([jnp.stack([tok, info[:, INFO_W1]], axis=-1),
                                 jnp.stack([tok, info[:, INFO_W2]], axis=-1)])
    per_row = jnp.zeros((n_rows, 2), F32).at[jnp.concatenate([pos1, pos2])].set(per_token)
    tok_of_row, row_w = per_row[:, 0].astype(jnp.int32), per_row[:, 1]
    n_used = cum_tiles[-1:].astype(jnp.int32)
    tile_expert = jnp.minimum(jnp.searchsorted(cum_tiles, jnp.arange(n_rows // rt, dtype=jnp.int32),
                                               side='right'), ne - 1).astype(jnp.int32)
    rows = _dispatch(h, tok_of_row, n_used, n_rows)
    grouped = functools.partial(_grouped_call, tile_expert=tile_expert, n_used=n_used)
    act = grouped(_moe_up_kernel, rows, [w1, w3], [row_w.reshape(n_rows, 1)],
                  tn=_pick(f, (512, 256, 128)), out_dtype=BF, name="moe_up")
    y_rows = grouped(_moe_down_kernel, act, [w2], [], tn=_pick(d, (1024, 512, 256, 128)), out_dtype=F32,
                     name="moe_down")
    return _combine(y_rows, pos1, pos2, m)


def _rope_tables(positions):
    inv_freq = ROPE_THETA ** (-jnp.arange(0, ROT_DIM, 2, dtype=F32) / ROT_DIM)
    ang = positions.astype(F32)[..., None] * inv_freq
    cos, sin = jnp.cos(ang), jnp.sin(ang)
    rest = HEAD_DIM - ROT_DIM
    ones = jnp.ones(ang.shape[:-1] + (rest,), F32)
    zeros = jnp.zeros(ang.shape[:-1] + (rest,), F32)
    z8 = jnp.zeros_like(sin)
    reps = LANES // HEAD_DIM
    flat = lambda t: jnp.tile(t, (1, 1, reps)).reshape(-1, LANES)
    cos_t = flat(jnp.concatenate([cos, cos, ones], axis=-1))
    sa_t = flat(jnp.concatenate([z8, sin, zeros], axis=-1))
    sb_t = flat(jnp.concatenate([-sin, z8, zeros], axis=-1))
    return cos_t, sa_t, sb_t


def _hybrid_mixer(h, rope, bsz, l, layer, w_in_all, sinks, conv_w, conv_b, dt_bias, a_log, ssd_d, ssd_norm,
                  lam_re, lam_im, log_step, b_re, b_im, c_re, c_im, s5_d, w_glu, b_glu, w_branch, w_out_all):
    m = h.shape[0]
    o0 = Q_W + 2 * KV_W
    o1 = o0 + SSD_INNER
    o2 = o1 + SSD_CONV_CH
    o3 = o2 + SSD_HEADS
    cast = lambda w: w.astype(BF)
    w_in = [(w_in_all, layer)]
    tm = min(1024, m)
    tab = [pl.BlockSpec((tm, LANES), lambda j, i: (i, 0))] * 3
    qkv = _mm_wres(h, w_in, _ep_qkv, n=o0, extras=rope, extra_specs=tab, name="qkv_proj_rope")
    z = _mm_wres(h, w_in, _ep_plain, n=SSD_INNER, col0=o0, name="z_proj")
    xbc = _mm_wres(h, w_in, _ep_plain, n=SSD_CONV_CH, col0=o1, name="xbc_proj")
    dt_raw = _mm_wres(h, w_in, _ep_plain, n=LANES, col0=o2, out_dtype=F32, name="dt_proj")
    o4 = o3 + S5_WIDTH
    u = _mm_wres(h, w_in, _ep_plain, n=S5_WIDTH, col0=o3, out_dtype=F32, name="u_proj")
    gates = _mm_wres(h, w_in, _ep_sigmoid, n=w_in_all.shape[2] - o4, col0=o4, name="gate_proj")

    o_attn = _attention(qkv, sinks, bsz, l)
    o_ssd = _ssd(z, xbc, dt_raw, conv_w, conv_b, dt_bias, a_log, ssd_d, ssd_norm, bsz, l)
    ops = _s5_operators(lam_re, lam_im, log_step, b_re, b_im, c_re, c_im, l // S5_CHUNK)
    y_s5 = _s5_scan(u.reshape(bsz, l, S5_WIDTH), ops, bsz, l)
    o_s5 = _s5_glu(y_s5, u, s5_d, cast(w_glu), b_glu)

    wb = cast(w_branch)
    merged = _branch_merge(o_attn, o_ssd, o_s5, wb[:Q_W], wb[Q_W:Q_W + SSD_INNER], wb[Q_W + SSD_INNER:], gates)
    return _mm_wres(merged, [(w_out_all, layer)], _ep_plain, n=w_out_all.shape[2], name="out_proj")


def kernel(x, c, positions, w_mod, b_mod, norm_mix_pre, norm_mix_post, norm_ffn_pre, norm_ffn_post, w_in, attn_sinks, conv_w, conv_b, dt_bias, a_log, ssd_d, ssd_norm, s5_lam_re, s5_lam_im, s5_log_step, s5_b_re, s5_b_im, s5_c_re, s5_c_im, s5_d, s5_w_glu, s5_b_glu, w_branch, w_out, ffn_w1, ffn_w3, ffn_w2, w_router, moe_w1, moe_w3, moe_w2):
    bsz, l, d = x.shape
    depth = w_mod.shape[0]
    assert l % SSD_CHUNK == 0 and l % ATTN_BLOCK == 0 and l % S5_CHUNK == 0
    rope = _rope_tables(positions)
    mod = _modulation(c, w_mod, b_mod)
    cast = lambda w: w.astype(BF)

    def mods(i):
        return [mod[i, :, k * d:(k + 1) * d] for k in range(6)]

    sh1, sc1, g1, sh2, sc2, g2 = mods(0)
    h = _pre_norm(x, norm_mix_pre[0], sh1, sc1)
    for i in range(depth):
        m = _hybrid_mixer(h.reshape(bsz * l, d), rope, bsz, l, i, w_in, attn_sinks[i], conv_w[i], conv_b[i],
                          dt_bias[i], a_log[i], ssd_d[i], ssd_norm[i], s5_lam_re[i], s5_lam_im[i],
                          s5_log_step[i], s5_b_re[i], s5_b_im[i], s5_c_re[i], s5_c_im[i], s5_d[i],
                          s5_w_glu[i], s5_b_glu[i], w_branch[i], w_out)
        is_moe = i % 2 == 1
        x, h = _post_norm(m, x, g1, norm_mix_post[i], norm_ffn_pre[i], sh2, sc2, h_dtype=F32 if is_moe else BF)
        h2 = h.reshape(bsz * l, d)
        j = i // 2
        if i % 2 == 0:
            act = _mm_wres(h2, [(ffn_w1, j), (ffn_w3, j)], _ep_swiglu, n=ffn_w1.shape[2], tn=256,
                           name="swiglu_up")
            f = _matmul(act, cast(ffn_w2[j]), tm=512, name="ffn_down")
        else:
            f = _moe(h2, w_router[j], moe_w1[j], moe_w3[j], moe_w2[j])
        if i + 1 < depth:
            sh1, sc1, g1n, sh2n, sc2n, g2n = mods(i + 1)
            x, h = _post_norm(f, x, g2, norm_ffn_post[i], norm_mix_pre[i + 1], sh1, sc1)
            g1, sh2, sc2, g2 = g1n, sh2n, sc2n, g2n
        else:
            x, _ = _post_norm(f, x, g2, norm_ffn_post[i])
    return x
```

```python
import functools
import math

import jax
import jax.numpy as jnp
import numpy as np
from jax import lax
from jax.experimental import pallas as pl
from jax.experimental.pallas import tpu as pltpu

BF = jnp.bfloat16
F32 = jnp.float32

HEAD_DIM = 64
Q_HEADS = 32
KV_HEADS = 4
Q_GROUP = Q_HEADS // KV_HEADS
Q_W = Q_HEADS * HEAD_DIM
KV_W = KV_HEADS * HEAD_DIM
ATTN_BLOCK = 128
ROT_DIM = 16
ROT_HALF = ROT_DIM // 2
ROPE_THETA = 500000.0
SSD_INNER = 2048
SSD_HEADS = 32
SSD_HEAD_DIM = 64
SSD_GROUPS = 4
SSD_STATE = 128
SSD_CONV = 4
SSD_CHUNK = 128
SSD_CONV_CH = SSD_INNER + 2 * SSD_GROUPS * SSD_STATE
S5_WIDTH = 1536
S5_GROUP_SIZE = 16
S5_GROUPS = 96
S5_STATE = 64
S5_CHUNK = 16
S5_LANE_GROUPS = 8
N_EXPERTS = 8
RMS_EPS = 1e-6
LANES = 128
MXU_COLS = 256
MIB = 1024 * 1024
NEG_BIG = -1e30


def _cparams(semantics, vmem_bytes):
    limit = int(min(max(vmem_bytes * 1.25 + 4 * MIB, 16 * MIB), 60 * MIB))
    return pltpu.CompilerParams(dimension_semantics=semantics, vmem_limit_bytes=limit)


def _pick(n, candidates):
    for c in candidates:
        if n % c == 0:
            return c
    raise ValueError(f"no tile in {candidates} divides {n}")


def _silu(x):
    return x * jax.nn.sigmoid(x)


def _mod_kernel(c_ref, w_ref, b_ref, o_ref):
    c = c_ref[...]
    a = _silu(c).astype(BF)
    o_ref[0] = jnp.dot(a, w_ref[0].astype(BF), preferred_element_type=F32) + b_ref[0]


def _modulation(c, w_mod, b_mod):
    depth, d, n = w_mod.shape
    bsz = c.shape[0]
    rows = 8
    c_pad = jnp.zeros((rows, d), F32).at[:bsz].set(c)
    tn = _pick(n, (512, 256, 128))
    out = pl.pallas_call(
        _mod_kernel,
        out_shape=jax.ShapeDtypeStruct((depth, rows, n), F32),
        grid=(depth, n // tn),
        in_specs=[pl.BlockSpec((rows, d), lambda l, j: (0, 0)),
                  pl.BlockSpec((1, d, tn), lambda l, j: (l, 0, j)),
                  pl.BlockSpec((1, 1, tn), lambda l, j: (l, 0, j))],
        out_specs=pl.BlockSpec((1, rows, tn), lambda l, j: (l, 0, j)),
        compiler_params=_cparams(("parallel", "arbitrary"), 2 * d * tn * 4 + d * tn * 2),
        name="adaln_modulation",
    )(c_pad, w_mod, b_mod.reshape(depth, 1, n))
    return out[:, :bsz]


def _rms(x, w):
    return x * lax.rsqrt(jnp.mean(x * x, axis=-1, keepdims=True) + RMS_EPS) * w


def _pre_norm_kernel(x_ref, w_ref, sh_ref, sc_ref, h_ref):
    h = _rms(x_ref[0], w_ref[...])
    h_ref[0] = (h * (1.0 + sc_ref[0]) + sh_ref[0]).astype(h_ref.dtype)


def _pre_norm(x, w, shift, scale):
    bsz, l, d = x.shape
    tl = _pick(l, (256, 128))
    vec = pl.BlockSpec((1, 1, d), lambda b, i: (b, 0, 0))
    return pl.pallas_call(
        _pre_norm_kernel,
        out_shape=jax.ShapeDtypeStruct((bsz, l, d), BF),
        grid=(bsz, l // tl),
        in_specs=[pl.BlockSpec((1, tl, d), lambda b, i: (b, i, 0)),
                  pl.BlockSpec((1, d), lambda b, i: (0, 0)), vec, vec],
        out_specs=pl.BlockSpec((1, tl, d), lambda b, i: (b, i, 0)),
        compiler_params=_cparams(("parallel", "parallel"), 2 * tl * d * 6 + 2 * tl * d * 4),
        name="pre_norm",
    )(x, w.reshape(1, d), shift.reshape(bsz, 1, d), scale.reshape(bsz, 1, d))


def _post_norm_kernel(m_ref, x_ref, g_ref, wpost_ref, wpre_ref, sh_ref, sc_ref, xo_ref, h_ref):
    xn = x_ref[0] + g_ref[0] * _rms(m_ref[0].astype(F32), wpost_ref[...])
    xo_ref[0] = xn
    h = _rms(xn, wpre_ref[...])
    h_ref[0] = (h * (1.0 + sc_ref[0]) + sh_ref[0]).astype(h_ref.dtype)


def _post_norm_last_kernel(m_ref, x_ref, g_ref, wpost_ref, xo_ref):
    xo_ref[0] = x_ref[0] + g_ref[0] * _rms(m_ref[0].astype(F32), wpost_ref[...])


def _post_norm(m, x, gate, w_post, w_pre=None, shift=None, scale=None, h_dtype=BF):
    bsz, l, d = x.shape
    tl = _pick(l, (256, 128))
    row = pl.BlockSpec((1, tl, d), lambda b, i: (b, i, 0))
    vec = pl.BlockSpec((1, 1, d), lambda b, i: (b, 0, 0))
    par = pl.BlockSpec((1, d), lambda b, i: (0, 0))
    m = m.reshape(bsz, l, d)
    gate = gate.reshape(bsz, 1, d)
    cp = _cparams(("parallel", "parallel"), 2 * tl * d * (2 + 4 + 4 + 2) + 3 * tl * d * 4)
    if w_pre is None:
        return pl.pallas_call(
            _post_norm_last_kernel,
            out_shape=jax.ShapeDtypeStruct((bsz, l, d), F32),
            grid=(bsz, l // tl),
            in_specs=[row, row, vec, par],
            out_specs=row,
            compiler_params=cp,
            name="post_norm_last",
        )(m, x, gate, w_post.reshape(1, d)), None
    return pl.pallas_call(
        _post_norm_kernel,
        out_shape=(jax.ShapeDtypeStruct((bsz, l, d), F32), jax.ShapeDtypeStruct((bsz, l, d), h_dtype)),
        grid=(bsz, l // tl),
        in_specs=[row, row, vec, par, par, vec, vec],
        out_specs=(row, row),
        compiler_params=cp,
        name="post_norm",
    )(m, x, gate, w_post.reshape(1, d), w_pre.reshape(1, d), shift.reshape(bsz, 1, d),
      scale.reshape(bsz, 1, d))


def _mm_call(body, a_list, w_list, extras, extra_specs, *, n, tm, tn, out_dtype, name):
    m = a_list[0].shape[0]
    tm = min(tm, m)
    assert m % tm == 0 and n % tn == 0
    a_specs = [pl.BlockSpec((tm, a.shape[1]), lambda i, j: (i, 0)) for a in a_list]
    w_specs = [pl.BlockSpec((w.shape[0], tn), lambda i, j: (0, j)) for w in w_list]
    vmem = sum(2 * tm * a.shape[1] * a.dtype.itemsize for a in a_list)
    vmem += sum(2 * w.shape[0] * tn * w.dtype.itemsize for w in w_list)
    vmem += 2 * tm * tn * 4 * (1 + len(extras)) + 3 * tm * tn * 4
    return pl.pallas_call(
        body,
        out_shape=jax.ShapeDtypeStruct((m, n), out_dtype),
        grid=(m // tm, n // tn),
        in_specs=a_specs + w_specs + list(extra_specs(tm, tn)),
        out_specs=pl.BlockSpec((tm, tn), lambda i, j: (i, j)),
        compiler_params=_cparams(("parallel", "arbitrary"), vmem),
        name=name,
    )(*a_list, *w_list, *extras)


def _dot(a, w):
    return jnp.dot(a, w, preferred_element_type=F32)


def _plain_body(a_ref, w_ref, o_ref):
    o_ref[...] = _dot(a_ref[...], w_ref[...]).astype(o_ref.dtype)


def _matmul(a, w, *, tn=None, tm=1024, out_dtype=BF, body=_plain_body, name="matmul"):
    n = w.shape[1]
    tn = tn or _pick(n, (512, 256, 128))
    return _mm_call(body, [a], [w], [], lambda tm_, tn_: [], n=n, tm=tm, tn=tn,
                    out_dtype=out_dtype, name=name)


def _wres_kernel(n_w, shift, epilogue, a_ref, *refs):
    n_in = n_w * (2 if shift else 1)
    w_refs, nxt_refs = refs[:n_w], refs[n_w:n_in]
    extras, o_ref, wb = refs[n_in:-n_w - 1], refs[-n_w - 1], refs[-n_w:]

    @pl.when(pl.program_id(1) == 0)
    def _():
        for idx, (w_ref, b) in enumerate(zip(w_refs, wb)):
            w = w_ref[...]
            if shift:
                w = jnp.concatenate([w[:, shift:], nxt_refs[idx][...][:, :shift]], axis=1)
            b[...] = w.astype(BF)

    a = a_ref[...]
    tn = o_ref.shape[1]
    sub = MXU_COLS if tn % MXU_COLS == 0 else tn
    for c in range(0, tn, sub):
        epilogue([_dot(a, b[:, c:c + sub]) for b in wb], extras, o_ref.at[:, c:c + sub], c, tn)


def _mm_wres(a, weights, epilogue, *, n, col0=0, tn=None, extras=(), extra_specs=(), tm=1024,
             out_dtype=BF, name):
    m, k = a.shape
    tm = min(tm, m)
    shift = col0 % LANES
    base = col0 - shift
    tn = tn or next(t for t in (512, 256, 128) if n % t == 0 and base % t == 0)
    assert m % tm == 0 and n % tn == 0 and base % tn == 0
    c0, per = base // tn, tn // LANES
    w_specs = [pl.BlockSpec((k, tn), functools.partial(lambda lead, j, i: (lead, c0 + j), lead))
               for _, lead in weights]
    if shift:
        w_specs += [pl.BlockSpec((k, LANES), functools.partial(lambda lead, j, i: (lead, (c0 + j + 1) * per), lead))
                    for _, lead in weights]
    w_args = [w.reshape(-1, w.shape[-1]) for w, _ in weights] * (2 if shift else 1)
    vmem = 2 * tm * k * 2 + len(weights) * (2 * k * (tn + LANES) * 4 + 2 * k * tn * 2) + 2 * tm * tn * 4
    vmem += 4 * tm * tn * 4
    return pl.pallas_call(
        functools.partial(_wres_kernel, len(weights), shift, epilogue),
        out_shape=jax.ShapeDtypeStruct((m, n), out_dtype),
        grid=(n // tn, m // tm),
        in_specs=[pl.BlockSpec((tm, k), lambda j, i: (i, 0))] + w_specs + list(extra_specs),
        out_specs=pl.BlockSpec((tm, tn), lambda j, i: (i, j)),
        scratch_shapes=[pltpu.VMEM((k, tn), BF)] * len(weights),
        compiler_params=_cparams(("arbitrary", "arbitrary"), vmem),
        name=name,
    )(a, *w_args, *extras)


def _ep_plain(accs, extras, o_ref, col, tile_w):
    o_ref[...] = accs[0].astype(o_ref.dtype)


def _ep_sigmoid(accs, extras, o_ref, col, tile_w):
    o_ref[...] = jax.nn.sigmoid(accs[0]).astype(o_ref.dtype)


def _ep_swiglu(accs, extras, o_ref, col, tile_w):
    o_ref[...] = (_silu(accs[0]) * accs[1]).astype(o_ref.dtype)


def _ep_qkv(accs, extras, o_ref, col, tile_w):
    cos_ref, sa_ref, sb_ref = extras
    acc = accs[0]
    first = pl.program_id(0) * tile_w + col
    cols = []
    for c in range(acc.shape[1] // LANES):
        r = acc[:, c * LANES:(c + 1) * LANES]
        g0 = first + c * LANES
        roped = (r * cos_ref[...] + pltpu.roll(r, ROT_HALF, axis=1) * sa_ref[...]
                 + pltpu.roll(r, LANES - ROT_HALF, axis=1) * sb_ref[...])
        r = jnp.where(g0 < Q_W + KV_W, roped, r)
        cols.append(r * jnp.where(g0 < Q_W, HEAD_DIM ** -0.5, 1.0))
    o_ref[...] = jnp.concatenate(cols, axis=1).astype(o_ref.dtype)


def _merge_body(oa_ref, ob_ref, oc_ref, wa_ref, wb_ref, wc_ref, ga_ref, gb_ref, gc_ref, o_ref):
    acc = ga_ref[...].astype(F32) * _dot(oa_ref[...], wa_ref[...])
    acc += gb_ref[...].astype(F32) * _dot(ob_ref[...], wb_ref[...])
    acc += gc_ref[...].astype(F32) * _dot(oc_ref[...], wc_ref[...])
    o_ref[...] = acc.astype(o_ref.dtype)


def _branch_merge(o_attn, o_ssd, o_s5, wa, wb, wc, gates):
    d = wa.shape[1]
    tn = _pick(d, (512, 256, 128))
    nt = d // tn

    def gate_specs(tm_, tn_):
        return [pl.BlockSpec((tm_, tn_), functools.partial(lambda br, i, j: (i, br * nt + j), br))
                for br in range(3)]

    return _mm_call(_merge_body, [o_attn, o_ssd, o_s5], [wa, wb, wc], [gates, gates, gates],
                    gate_specs, n=d, tm=1024, tn=tn, out_dtype=BF, name="branch_merge")


def _attn_kernel(sink_ref, q_ref, kc_ref, kp_ref, vc_ref, vp_ref, o_ref):
    n = pl.program_id(1)
    qi = lax.broadcasted_iota(jnp.int32, (ATTN_BLOCK, 2 * ATTN_BLOCK), 0)
    sj = lax.broadcasted_iota(jnp.int32, (ATTN_BLOCK, 2 * ATTN_BLOCK), 1)
    rel = qi + ATTN_BLOCK - sj
    valid = (rel >= 0) & (rel < ATTN_BLOCK) & ((sj >= ATTN_BLOCK) | (n > 0))
    q = q_ref[0]
    kv = 2 * ATTN_BLOCK
    pair_w = 2 * HEAD_DIM
    lane = lax.broadcasted_iota(jnp.int32, (ATTN_BLOCK, pair_w), 1)
    first = lane < HEAD_DIM
    zeros = jnp.zeros((kv, HEAD_DIM), BF)
    ones = jnp.ones((kv, HEAD_DIM), BF)
    outs = []
    for g in range(KV_HEADS):
        sl = slice(g * HEAD_DIM, (g + 1) * HEAD_DIM)
        kbt = jnp.concatenate([kp_ref[0, g], kc_ref[0, g]], axis=1)
        vb = jnp.concatenate([vp_ref[0][:, sl], vc_ref[0][:, sl]], axis=0)
        pv = jnp.concatenate([jnp.concatenate([vb, zeros, ones, zeros], axis=1),
                              jnp.concatenate([zeros, vb, zeros, ones], axis=1)], axis=0)
        for r in range(0, Q_GROUP, 2):
            ps, sinks = [], []
            for k in range(2):
                hd = g * Q_GROUP + r + k
                s = _dot(q[:, hd * HEAD_DIM:(hd + 1) * HEAD_DIM], kbt)
                s = jnp.where(valid, s, NEG_BIG)
                mx = jnp.maximum(jnp.max(s, axis=-1, keepdims=True), sink_ref[hd])
                ps.append(jnp.exp(s - mx).astype(BF))
                sinks.append(jnp.exp(sink_ref[hd] - mx))
            res = _dot(jnp.concatenate(ps, axis=1), pv)
            den = res[:, pair_w:] + jnp.where(first, sinks[0], sinks[1])
            outs.append(res[:, :pair_w] / den)
    o_ref[0] = jnp.concatenate(outs, axis=1).astype(o_ref.dtype)


def _attention(qkv, sinks, bsz, l):
    nb = l // ATTN_BLOCK
    qkv = qkv.reshape(bsz, l, Q_W + 2 * KV_W)
    k_t = qkv[:, :, Q_W:Q_W + KV_W].reshape(bsz, l, KV_HEADS, HEAD_DIM).transpose(0, 2, 3, 1)
    vcol = Q_W // KV_W + 1
    kspec = lambda idx: pl.BlockSpec((1, KV_HEADS, HEAD_DIM, ATTN_BLOCK), lambda b, n: (b, 0, 0, idx(n)))
    vspec = lambda idx: pl.BlockSpec((1, ATTN_BLOCK, KV_W), lambda b, n: (b, idx(n), vcol))
    cur = lambda n: n
    prev = lambda n: jnp.maximum(n - 1, 0)
    out = pl.pallas_call(
        _attn_kernel,
        out_shape=jax.ShapeDtypeStruct((bsz, l, Q_W), BF),
        grid=(bsz, nb),
        in_specs=[pl.BlockSpec(memory_space=pltpu.SMEM),
                  pl.BlockSpec((1, ATTN_BLOCK, Q_W), lambda b, n: (b, n, 0)),
                  kspec(cur), kspec(prev), vspec(cur), vspec(prev)],
        out_specs=pl.BlockSpec((1, ATTN_BLOCK, Q_W), lambda b, n: (b, n, 0)),
        compiler_params=_cparams(("parallel", "parallel"), 8 * MIB),
        name="sliding_window_attention",
    )(sinks.astype(F32), qkv, k_t, k_t, qkv, qkv)
    return out.reshape(bsz * l, Q_W)


def _split_dot(f, e):
    hi = f.astype(BF)
    lo = (f - hi.astype(F32)).astype(BF)
    return _dot(hi, e) + _dot(lo, e)


def _ssd_kernel(z_ref, xbc_ref, dt_ref, cw_ref, cb_ref, dtb_ref, alog_ref, dsk_ref, nw_ref,
                o_ref, ext_ref, state_ref):
    c = pl.program_id(1)
    q = SSD_CHUNK
    halo = 8

    @pl.when(c == 0)
    def _():
        ext_ref[0:halo, :] = jnp.zeros((halo, SSD_CONV_CH), F32)
        state_ref[...] = jnp.zeros_like(state_ref)

    ext_ref[halo:halo + q, :] = xbc_ref[0].astype(F32)
    conv = cb_ref[...] + cw_ref[SSD_CONV - 1:SSD_CONV, :] * ext_ref[halo:halo + q, :]
    for j in range(SSD_CONV - 1):
        off = halo - (SSD_CONV - 1) + j
        conv = conv + cw_ref[j:j + 1, :] * ext_ref[off:off + q, :]
    tail = ext_ref[q:q + halo, :]
    ext_ref[0:halo, :] = tail
    xbc = _silu(conv)
    xs = xbc[:, :SSD_INNER]
    xs_bf = xs.astype(BF)
    bm = xbc[:, SSD_INNER:SSD_INNER + SSD_GROUPS * SSD_STATE].astype(BF)
    cm = xbc[:, SSD_INNER + SSD_GROUPS * SSD_STATE:].astype(BF)

    dt = jax.nn.softplus(dt_ref[0] + dtb_ref[...])
    a = -jnp.exp(alog_ref[...])
    cs = dt * a
    row = lax.broadcasted_iota(jnp.int32, (q, LANES), 0)
    sh = 1
    while sh < q:
        cs = cs + jnp.where(row >= sh, pltpu.roll(cs, sh, axis=0), 0.0)
        sh *= 2
    cs_last = cs[q - 1:q, :]
    ecs = jnp.exp(cs)
    dte = jnp.exp(cs_last - cs) * dt
    cs_t = cs.T
    dt_t = dt.T

    hl = lax.broadcasted_iota(jnp.int32, (LANES, SSD_INNER), 0)
    hc = lax.broadcasted_iota(jnp.int32, (LANES, SSD_INNER), 1)
    expand = jnp.where(hl == hc // SSD_HEAD_DIM, 1.0, 0.0).astype(BF)
    ecs_x = _split_dot(ecs, expand)
    dte_x = _split_dot(dte, expand)

    li = lax.broadcasted_iota(jnp.int32, (q, q), 0)
    si = lax.broadcasted_iota(jnp.int32, (q, q), 1)
    causal = li >= si
    heads_per_group = SSD_HEADS // SSD_GROUPS
    gw = heads_per_group * SSD_HEAD_DIM
    y_parts = []
    for g in range(SSD_GROUPS):
        ns = slice(g * SSD_STATE, (g + 1) * SSD_STATE)
        gs = slice(g * gw, (g + 1) * gw)
        bm_g, cm_g = bm[:, ns], cm[:, ns]
        cb = lax.dot_general(cm_g, bm_g, (((1,), (1,)), ((), ())), preferred_element_type=F32)
        for r in range(heads_per_group):
            hd = g * heads_per_group + r
            seg = cs[:, hd:hd + 1] - cs_t[hd:hd + 1, :]
            decay = jnp.exp(jnp.where(causal, seg, NEG_BIG))
            w = (cb * decay * dt_t[hd:hd + 1, :]).astype(BF)
            y_parts.append(_dot(w, xs_bf[:, hd * SSD_HEAD_DIM:(hd + 1) * SSD_HEAD_DIM]))
    y = jnp.concatenate(y_parts, axis=1)
    xw = (xs * dte_x).astype(BF)
    for g in range(SSD_GROUPS):
        ns = slice(g * SSD_STATE, (g + 1) * SSD_STATE)
        gs = slice(g * gw, (g + 1) * gw)
        prev = state_ref[:, gs]
        y_off = _dot(cm[:, ns], prev.astype(BF)) * ecs_x[:, gs]
        upd = lax.dot_general(bm[:, ns], xw[:, gs], (((0,), (0,)), ((), ())),
                              preferred_element_type=F32)
        state_ref[:, gs] = prev * ecs_x[q - 1:q, gs] + upd
        yg = y[:, gs] + y_off + dsk_ref[:, gs] * xs[:, gs]
        yg = yg * _silu(z_ref[0, :, gs].astype(F32))
        yg = yg * lax.rsqrt(jnp.mean(yg * yg, axis=-1, keepdims=True) + RMS_EPS)
        o_ref[0, :, gs] = (yg * nw_ref[:, gs]).astype(o_ref.dtype)


def _ssd(z, xbc, dt_raw, conv_w, conv_b, dt_bias, a_log, d_skip, norm_w, bsz, l):
    q = SSD_CHUNK
    pad = LANES - SSD_HEADS
    row = lambda w: pl.BlockSpec((1, q, w), lambda b, c: (b, c, 0))
    par = lambda r, w: pl.BlockSpec((r, w), lambda b, c: (0, 0))
    out = pl.pallas_call(
        _ssd_kernel,
        out_shape=jax.ShapeDtypeStruct((bsz, l, SSD_INNER), BF),
        grid=(bsz, l // q),
        in_specs=[row(SSD_INNER), row(SSD_CONV_CH), row(LANES),
                  par(SSD_CONV, SSD_CONV_CH), par(1, SSD_CONV_CH), par(1, LANES), par(1, LANES),
                  par(1, SSD_INNER), par(1, SSD_INNER)],
        out_specs=row(SSD_INNER),
        scratch_shapes=[pltpu.VMEM((q + 8, SSD_CONV_CH), F32), pltpu.VMEM((SSD_STATE, SSD_INNER), F32)],
        compiler_params=_cparams(("parallel", "arbitrary"), 24 * MIB),
        name="ssd_chunked",
    )(z.reshape(bsz, l, SSD_INNER), xbc.reshape(bsz, l, SSD_CONV_CH), dt_raw.reshape(bsz, l, LANES),
      conv_w, conv_b.reshape(1, -1), jnp.pad(dt_bias, (0, pad)).reshape(1, LANES),
      jnp.pad(a_log, (0, pad)).reshape(1, LANES),
      jnp.repeat(d_skip, SSD_HEAD_DIM).reshape(1, SSD_INNER), norm_w.reshape(1, SSD_INNER))
    return out.reshape(bsz * l, SSD_INNER)


def _s5_operators(lam_re, lam_im, log_step, b_re, b_im, c_re, c_im, n_chunks):
    hp = lax.Precision.HIGHEST
    qn = S5_CHUNK
    lr, li = lam_re.astype(F32), lam_im.astype(F32)
    step = jnp.exp(log_step.astype(F32))[:, None]
    mag = jnp.exp(lr * step)
    a_re, a_im = mag * jnp.cos(li * step), mag * jnp.sin(li * step)
    den = lr * lr + li * li
    coef_re = ((a_re - 1.0) * lr + a_im * li) / den
    coef_im = (a_im * lr - (a_re - 1.0) * li) / den
    br, bi = b_re.astype(F32), b_im.astype(F32)
    bb_re = coef_re[..., None] * br - coef_im[..., None] * bi
    bb_im = coef_re[..., None] * bi + coef_im[..., None] * br

    def power(d):
        d = d.astype(F32)[None, :, None]
        m = jnp.exp(lr[:, None, :] * step[:, None, :] * d)
        ang = li[:, None, :] * step[:, None, :] * d
        return m * jnp.cos(ang), m * jnp.sin(ang)

    p_re, p_im = power(jnp.arange(qn + 1))
    cr, ci = c_re.astype(F32), c_im.astype(F32)
    m_re = cr[:, None] * p_re[:, :, None, :] - ci[:, None] * p_im[:, :, None, :]
    m_im = cr[:, None] * p_im[:, :, None, :] + ci[:, None] * p_re[:, :, None, :]
    kern = (jnp.einsum('gdip,gpj->gdij', m_re, bb_re, precision=hp)
            - jnp.einsum('gdip,gpj->gdij', m_im, bb_im, precision=hp))
    gl, nlg, gs, ps = S5_LANE_GROUPS, S5_GROUPS // S5_LANE_GROUPS, S5_GROUP_SIZE, S5_STATE
    lag_op = kern[:, :qn].reshape(nlg, gl, qn, gs, gs).transpose(0, 1, 4, 2, 3)
    lag_op = lag_op.reshape(nlg, LANES, qn * gs)
    e_re, e_im = p_re[:, qn - 1 - jnp.arange(qn)], p_im[:, qn - 1 - jnp.arange(qn)]
    bt_re, bt_im = bb_re.transpose(0, 2, 1)[:, None], bb_im.transpose(0, 2, 1)[:, None]
    be = jnp.stack([e_re[:, :, None, :] * bt_re - e_im[:, :, None, :] * bt_im,
                    e_re[:, :, None, :] * bt_im + e_im[:, :, None, :] * bt_re], axis=3)
    b_end = be.reshape(nlg, gl, qn, gs, 2 * ps).transpose(0, 2, 1, 3, 4)
    b_end = b_end.reshape(nlg, qn * LANES, 2 * ps)
    ci = jnp.stack([m_re[:, 1:].transpose(0, 3, 1, 2), -m_im[:, 1:].transpose(0, 3, 1, 2)], axis=1)
    c_in = ci.reshape(nlg, gl, 2, ps, qn * gs).transpose(0, 2, 1, 3, 4)
    c_in = c_in.reshape(nlg, 2 * gl * ps, qn * gs)
    n_steps = max(1, (n_chunks - 1).bit_length())
    s_re, s_im = power(qn * (2 ** jnp.arange(n_steps)))
    lanes = lambda t: t.reshape(nlg, gl, n_steps, ps).transpose(0, 2, 1, 3).reshape(nlg, n_steps, gl * ps)
    pw1 = jnp.concatenate([lanes(s_re), lanes(s_re)], axis=-1)
    pw2 = jnp.concatenate([-lanes(s_im), lanes(s_im)], axis=-1)
    return lag_op.astype(BF), b_end.astype(BF), c_in.astype(BF), pw1, pw2


def _s5_kernel(u_ref, lag_ref, bend_ref, cin_ref, t_out_ref, t_state_ref, pw1_ref, pw2_ref, y_ref,
               toep_ref, bend_full, cin_full, ucat_ref):
    qn, gs, ps = S5_CHUNK, S5_GROUP_SIZE, S5_STATE
    n_chunks = u_ref.shape[1] // qn
    n_steps = pw1_ref.shape[1]
    st = pw1_ref.shape[2]
    half = st // 2

    @pl.when(pl.program_id(1) == 0)
    def _():
        blk = lambda d: slice(d * LANES, (d + 1) * LANES)
        col = lambda n: lax.broadcasted_iota(jnp.int32, (n, 1), 0)
        out_group = lax.broadcasted_iota(jnp.int32, (1, LANES), 1) // gs
        lag_diag = col(LANES) // gs == out_group
        cin_diag = (col(st) % half) // ps == out_group
        toep_ref[...] = jnp.zeros_like(toep_ref)
        for d in range(qn):
            lag_d = jnp.where(lag_diag, _dot(lag_ref[0], t_out_ref[:, blk(d)]), 0.0).astype(BF)
            for s in range(qn - d):
                toep_ref[blk(s), blk(s + d)] = lag_d
            cin_full[:, blk(d)] = jnp.where(cin_diag, _dot(cin_ref[0], t_out_ref[:, blk(d)]), 0.0).astype(BF)
        in_group = (col(qn * LANES) % LANES) // gs
        for k in range(st // LANES):
            state_group = ((k * LANES + lax.broadcasted_iota(jnp.int32, (1, LANES), 1)) % half) // ps
            bend_full[:, blk(k)] = jnp.where(in_group == state_group,
                                             _dot(bend_ref[0], t_state_ref[:, blk(k)]), 0.0).astype(BF)

    for s in range(qn):
        ucat_ref[:, s * LANES:(s + 1) * LANES] = u_ref[0, pl.ds(s, n_chunks, stride=qn), :].astype(BF)
    ucat = ucat_ref[...]
    x = _dot(ucat, bend_full[...])
    row = lax.broadcasted_iota(jnp.int32, x.shape, 0)
    for k in range(n_steps):
        sh = 1 << k
        xs = jnp.where(row >= sh, pltpu.roll(x, sh, axis=0), 0.0)
        x = x + pw1_ref[0, k:k + 1, :] * xs + pw2_ref[0, k:k + 1, :] * pltpu.roll(xs, half, axis=1)
    x_in = jnp.where(row >= 1, pltpu.roll(x, 1, axis=0), 0.0)
    y = _dot(ucat, toep_ref[...]) + _dot(x_in.astype(BF), cin_full[...])
    for t in range(qn):
        y_ref[0, pl.ds(t, n_chunks, stride=qn), :] = y[:, t * LANES:(t + 1) * LANES]


def _s5_scan(u, ops, bsz, l):
    lag_op, b_end, c_in, pw1, pw2 = ops
    qn, gs, ps = S5_CHUNK, S5_GROUP_SIZE, S5_STATE
    nlg = S5_WIDTH // LANES
    nc = l // qn
    wide = qn * LANES
    st = pw1.shape[2]
    r, c = np.arange(qn * gs)[:, None], np.arange(wide)[None, :]
    t_out = jnp.asarray((r // gs == c // LANES) & (r % gs == c % gs), BF)
    r, c = np.arange(2 * ps)[:, None], np.arange(st)[None, :]
    t_state = jnp.asarray((r // ps == c // (st // 2)) & (r % ps == c % ps), BF)
    wspec = lambda rows, cols: pl.BlockSpec((1, rows, cols), lambda g, b: (g, 0, 0))
    full = lambda a: pl.BlockSpec(a.shape, lambda g, b: (0, 0))
    seq = pl.BlockSpec((1, l, LANES), lambda g, b: (b, 0, g))
    y = pl.pallas_call(
        _s5_kernel,
        out_shape=jax.ShapeDtypeStruct((bsz, l, S5_WIDTH), F32),
        grid=(nlg, bsz),
        in_specs=[seq, wspec(LANES, qn * gs), wspec(wide, 2 * ps), wspec(st, qn * gs), full(t_out), full(t_state),
                  wspec(pw1.shape[1], st), wspec(pw1.shape[1], st)],
        out_specs=seq,
        scratch_shapes=[pltpu.VMEM((wide, wide), BF), pltpu.VMEM((wide, st), BF), pltpu.VMEM((st, wide), BF),
                        pltpu.VMEM((nc, wide), BF)],
        compiler_params=_cparams(("arbitrary", "arbitrary"),
                                 4 * l * LANES * 4 + wide * wide * 2 + 2 * wide * st * 2 + 4 * wide * LANES * 2
                                 + nc * wide * 2 + 3 * nc * wide * 4 + 4 * nc * st * 4),
        name="s5_chunked_scan",
    )(u, lag_op, b_end, c_in, t_out, t_state, pw1, pw2)
    return y.reshape(bsz * l, S5_WIDTH)


def _s5_glu_kernel(y_ref, u_ref, d_ref, w_ref, b_ref, o_ref):
    y = y_ref[...].astype(F32) + d_ref[...] * u_ref[...].astype(F32)
    v = jax.nn.gelu(y, approximate=True)
    o_ref[...] = (v * jax.nn.sigmoid(_dot(v.astype(BF), w_ref[...]) + b_ref[...])).astype(o_ref.dtype)


def _s5_glu(y, u, d_skip, w_glu, b_glu):
    m, w = y.shape
    tm = _pick(m, (512, 256, 128))
    row = pl.BlockSpec((tm, w), lambda i: (i, 0))
    par = pl.BlockSpec((1, w), lambda i: (0, 0))
    return pl.pallas_call(
        _s5_glu_kernel,
        out_shape=jax.ShapeDtypeStruct((m, w), BF),
        grid=(m // tm,),
        in_specs=[row, row, par, pl.BlockSpec((w, w), lambda i: (0, 0)), par],
        out_specs=row,
        compiler_params=_cparams(("parallel",), 4 * tm * w * 4 + 2 * tm * w * 2 + 2 * w * w * 2 + 4 * tm * w * 4),
        name="s5_gelu_glu",
    )(y, u, d_skip.reshape(1, w), w_glu, b_glu.reshape(1, w))


MOE_ROW_TILE = 256
INFO_E1, INFO_E2, INFO_W1, INFO_W2, INFO_R1, INFO_R2 = range(6)


def _split_bf16(x):
    hi = x.astype(BF)
    return hi, (x - hi.astype(F32)).astype(BF)


def _router_kernel(h_ref, w_ref, info_ref, cnt_ref, carry_ref):
    i = pl.program_id(0)

    @pl.when(i == 0)
    def _():
        carry_ref[...] = jnp.zeros_like(carry_ref)

    hh, hl = _split_bf16(h_ref[...])
    wh, wl = _split_bf16(w_ref[...])
    logits = _dot(hh, wh) + _dot(hl, wh) + _dot(hh, wl)
    tm = logits.shape[0]
    lane = lax.broadcasted_iota(jnp.int32, logits.shape, 1)
    lg = jnp.where(lane < N_EXPERTS, logits, -jnp.inf)
    m1 = jnp.max(lg, axis=-1, keepdims=True)
    i1 = jnp.min(jnp.where(lg == m1, lane, LANES), axis=-1, keepdims=True)
    lg2 = jnp.where(lane == i1, -jnp.inf, lg)
    m2 = jnp.max(lg2, axis=-1, keepdims=True)
    i2 = jnp.min(jnp.where(lg2 == m2, lane, LANES), axis=-1, keepdims=True)
    e = jnp.exp(m2 - m1)
    w1 = 1.0 / (1.0 + e)
    w2 = e * w1
    onehot = jnp.where(lane == i1, 1.0, 0.0) + jnp.where(lane == i2, 1.0, 0.0)
    rt = lax.broadcasted_iota(jnp.int32, (tm, tm), 0)
    ct = lax.broadcasted_iota(jnp.int32, (tm, tm), 1)
    before = jnp.where(rt > ct, 1.0, 0.0).astype(BF)
    rank = _dot(before, onehot.astype(BF)) + carry_ref[...]
    r1 = jnp.sum(jnp.where(lane == i1, rank, 0.0), axis=-1, keepdims=True)
    r2 = jnp.sum(jnp.where(lane == i2, rank, 0.0), axis=-1, keepdims=True)
    carry_ref[...] += jnp.sum(onehot, axis=0, keepdims=True)
    cnt_ref[...] = carry_ref[...]
    info = jnp.zeros_like(logits)
    for slot, val in ((INFO_E1, i1.astype(F32)), (INFO_E2, i2.astype(F32)), (INFO_W1, w1), (INFO_W2, w2),
                      (INFO_R1, r1), (INFO_R2, r2)):
        info = jnp.where(lane == slot, val, info)
    info_ref[...] = info


def _router(h, w_router):
    m, d = h.shape
    tm = _pick(m, (512, 256, 128))
    w_pad = jnp.zeros((d, LANES), F32).at[:, :N_EXPERTS].set(w_router.astype(F32))
    return pl.pallas_call(
        _router_kernel,
        out_shape=(jax.ShapeDtypeStruct((m, LANES), F32), jax.ShapeDtypeStruct((1, LANES), F32)),
        grid=(m // tm,),
        in_specs=[pl.BlockSpec((tm, d), lambda i: (i, 0)), pl.BlockSpec((d, LANES), lambda i: (0, 0))],
        out_specs=(pl.BlockSpec((tm, LANES), lambda i: (i, 0)), pl.BlockSpec((1, LANES), lambda i: (0, 0))),
        scratch_shapes=[pltpu.VMEM((1, LANES), F32)],
        compiler_params=_cparams(("arbitrary",), 2 * tm * d * 4 + 2 * tm * d * 2 + 2 * d * LANES * 4
                                 + 2 * tm * tm * 2 + 8 * tm * LANES * 4),
        name="moe_router",
    )(h, w_pad)


def _row_copy(src_hbm, src_row, buf, slot, dst_row, sem):
    return pltpu.make_async_copy(src_hbm.at[pl.ds(src_row, 1)], buf.at[slot, pl.ds(dst_row, 1)], sem.at[slot])


def _dispatch_kernel(tok_ref, nu_ref, h_hbm, o_ref, buf, sem):
    i = pl.program_id(0)
    rows = o_ref.shape[0]
    nu = nu_ref[0]

    def issue(tile, slot):
        def body(r, carry):
            _row_copy(h_hbm, tok_ref[tile * rows + r], buf, slot, r, sem).start()
            return carry
        lax.fori_loop(0, rows, body, 0, unroll=8)

    def wait(slot):
        def body(r, carry):
            _row_copy(h_hbm, 0, buf, slot, 0, sem).wait()
            return carry
        lax.fori_loop(0, rows, body, 0, unroll=8)

    @pl.when(i == 0)
    def _():
        issue(0, 0)

    @pl.when(i + 1 < nu)
    def _():
        issue(i + 1, (i + 1) % 2)

    @pl.when(i < nu)
    def _():
        wait(i % 2)
        o_ref[...] = buf[i % 2].astype(o_ref.dtype)

    @pl.when(i >= nu)
    def _():
        o_ref[...] = jnp.zeros_like(o_ref)


def _dispatch(h, tok_of_row, n_used, n_rows):
    d = h.shape[1]
    rt = MOE_ROW_TILE
    return pl.pallas_call(
        _dispatch_kernel,
        out_shape=jax.ShapeDtypeStruct((n_rows, d), BF),
        grid_spec=pltpu.PrefetchScalarGridSpec(
            num_scalar_prefetch=2,
            grid=(n_rows // rt,),
            in_specs=[pl.BlockSpec(memory_space=pl.ANY)],
            out_specs=pl.BlockSpec((rt, d), lambda i, tok, nu: (i, 0)),
            scratch_shapes=[pltpu.VMEM((2, rt, d), F32), pltpu.SemaphoreType.DMA((2,))]),
        compiler_params=_cparams(("arbitrary",), 2 * rt * d * 4 + 2 * rt * d * 2 + rt * d * 4),
        name="moe_dispatch",
    )(tok_of_row, n_used, h)


def _combine_kernel(p1_ref, p2_ref, y_hbm, o_ref, buf, sem):
    i = pl.program_id(0)
    n = pl.num_programs(0)
    rows = o_ref.shape[0]

    def issue(tile, slot):
        def body(r, carry):
            t = tile * rows + r
            _row_copy(y_hbm, p1_ref[t], buf, slot, r, sem).start()
            _row_copy(y_hbm, p2_ref[t], buf, slot, rows + r, sem).start()
            return carry
        lax.fori_loop(0, rows, body, 0, unroll=4)

    def wait(slot):
        def body(r, carry):
            _row_copy(y_hbm, 0, buf, slot, 0, sem).wait()
            return carry
        lax.fori_loop(0, 2 * rows, body, 0, unroll=8)

    @pl.when(i == 0)
    def _():
        issue(0, 0)

    @pl.when(i + 1 < n)
    def _():
        issue(i + 1, (i + 1) % 2)

    wait(i % 2)
    both = buf[i % 2]
    o_ref[...] = (both[:rows] + both[rows:]).astype(o_ref.dtype)


def _combine(y_rows, pos1, pos2, n_tokens):
    d = y_rows.shape[1]
    tt = 128
    return pl.pallas_call(
        _combine_kernel,
        out_shape=jax.ShapeDtypeStruct((n_tokens, d), BF),
        grid_spec=pltpu.PrefetchScalarGridSpec(
            num_scalar_prefetch=2,
            grid=(n_tokens // tt,),
            in_specs=[pl.BlockSpec(memory_space=pl.ANY)],
            out_specs=pl.BlockSpec((tt, d), lambda i, p1, p2: (i, 0)),
            scratch_shapes=[pltpu.VMEM((2, 2 * tt, d), F32), pltpu.SemaphoreType.DMA((2,))]),
        compiler_params=_cparams(("arbitrary",), 4 * tt * d * 4 + 2 * tt * d * 2 + 2 * tt * d * 4),
        name="moe_combine",
    )(pos1, pos2, y_rows)


def _expert_changed(te_ref, i):
    return jnp.logical_or(i == 0, te_ref[i] != te_ref[jnp.maximum(i - 1, 0)])


def _moe_up_kernel(te_ref, nu_ref, a_ref, w1_ref, w3_ref, rw_ref, o_ref, w1b, w3b):
    i = pl.program_id(1)
    valid = i < nu_ref[0]

    @pl.when(jnp.logical_and(valid, _expert_changed(te_ref, i)))
    def _():
        w1b[...] = w1_ref[0].astype(BF)
        w3b[...] = w3_ref[0].astype(BF)

    @pl.when(valid)
    def _():
        a = a_ref[...]
        tn = o_ref.shape[1]
        sub = MXU_COLS if tn % MXU_COLS == 0 else tn
        for c in range(0, tn, sub):
            act = _silu(_dot(a, w1b[:, c:c + sub])) * _dot(a, w3b[:, c:c + sub])
            o_ref[:, c:c + sub] = (act * rw_ref[...]).astype(o_ref.dtype)

    @pl.when(jnp.logical_not(valid))
    def _():
        o_ref[...] = jnp.zeros_like(o_ref)


def _moe_down_kernel(te_ref, nu_ref, a_ref, w_ref, o_ref, wb):
    i = pl.program_id(1)
    valid = i < nu_ref[0]

    @pl.when(jnp.logical_and(valid, _expert_changed(te_ref, i)))
    def _():
        wb[...] = w_ref[0].astype(BF)

    @pl.when(valid)
    def _():
        o_ref[...] = _dot(a_ref[...], wb[...]).astype(o_ref.dtype)

    @pl.when(jnp.logical_not(valid))
    def _():
        o_ref[...] = jnp.zeros_like(o_ref)


def _grouped_call(kern, rows_in, weights, extras, *, tn, out_dtype, tile_expert, n_used, name):
    n_rows, k = rows_in.shape
    n = weights[0].shape[2]
    rt = MOE_ROW_TILE
    last = lambda i, nu: jnp.minimum(i, nu[0] - 1)
    row_spec = lambda w: pl.BlockSpec((rt, w), lambda j, i, te, nu: (last(i, nu), 0))
    w_spec = pl.BlockSpec((1, k, tn), lambda j, i, te, nu: (te[last(i, nu)], 0, j))
    vmem = 2 * rt * k * 2 + len(weights) * (2 * k * tn * 4 + k * tn * 2) + 2 * rt * tn * 4 + 4 * rt * tn * 4
    return pl.pallas_call(
        kern,
        out_shape=jax.ShapeDtypeStruct((n_rows, n), out_dtype),
        grid_spec=pltpu.PrefetchScalarGridSpec(
            num_scalar_prefetch=2,
            grid=(n // tn, n_rows // rt),
            in_specs=[row_spec(k)] + [w_spec] * len(weights) + [row_spec(e.shape[1]) for e in extras],
            out_specs=pl.BlockSpec((rt, tn), lambda j, i, te, nu: (i, j)),
            scratch_shapes=[pltpu.VMEM((k, tn), BF)] * len(weights)),
        compiler_params=_cparams(("arbitrary", "arbitrary"), vmem),
        name=name,
    )(tile_expert, n_used, rows_in, *weights, *extras)


def _moe(h, w_router, w1, w3, w2):
    m, d = h.shape
    ne, _, f = w1.shape
    rt = MOE_ROW_TILE
    n_rows = 2 * m + ne * rt
    info, counts = _router(h, w_router)
    e1, e2 = info[:, INFO_E1].astype(jnp.int32), info[:, INFO_E2].astype(jnp.int32)
    r1, r2 = info[:, INFO_R1].astype(jnp.int32), info[:, INFO_R2].astype(jnp.int32)
    tiles = (counts[0, :ne].astype(jnp.int32) + rt - 1) // rt
    cum_tiles = jnp.cumsum(tiles)
    base = (cum_tiles - tiles) * rt
    pos1, pos2 = base[e1] + r1, base[e2] + r2
    tok = jnp.arange(m, dtype=F32)
    assert m < 2 ** 24
    per_token = jnp.concatenate([jnp.stack([tok, info[:, INFO_W1]], axis=-1),
                                 jnp.stack([tok, info[:, INFO_W2]], axis=-1)])
    per_row = jnp.zeros((n_rows, 2), F32).at[jnp.concatenate([pos1, pos2])].set(per_token)
    tok_of_row, row_w = per_row[:, 0].astype(jnp.int32), per_row[:, 1]
    n_used = cum_tiles[-1:].astype(jnp.int32)
    tile_expert = jnp.minimum(jnp.searchsorted(cum_tiles, jnp.arange(n_rows // rt, dtype=jnp.int32),
                                               side='right'), ne - 1).astype(jnp.int32)
    rows = _dispatch(h, tok_of_row, n_used, n_rows)
    grouped = functools.partial(_grouped_call, tile_expert=tile_expert, n_used=n_used)
    act = grouped(_moe_up_kernel, rows, [w1, w3], [row_w.reshape(n_rows, 1)],
                  tn=_pick(f, (512, 256, 128)), out_dtype=BF, name="moe_up")
    y_rows = grouped(_moe_down_kernel, act, [w2], [], tn=_pick(d, (1024, 512, 256, 128)), out_dtype=F32,
                     name="moe_down")
    return _combine(y_rows, pos1, pos2, m)


def _rope_tables(positions):
    inv_freq = ROPE_THETA ** (-jnp.arange(0, ROT_DIM, 2, dtype=F32) / ROT_DIM)
    ang = positions.astype(F32)[..., None] * inv_freq
    cos, sin = jnp.cos(ang), jnp.sin(ang)
    rest = HEAD_DIM - ROT_DIM
    ones = jnp.ones(ang.shape[:-1] + (rest,), F32)
    zeros = jnp.zeros(ang.shape[:-1] + (rest,), F32)
    z8 = jnp.zeros_like(sin)
    reps = LANES // HEAD_DIM
    flat = lambda t: jnp.tile(t, (1, 1, reps)).reshape(-1, LANES)
    cos_t = flat(jnp.concatenate([cos, cos, ones], axis=-1))
    sa_t = flat(jnp.concatenate([z8, sin, zeros], axis=-1))
    sb_t = flat(jnp.concatenate([-sin, z8, zeros], axis=-1))
    return cos_t, sa_t, sb_t


def _hybrid_mixer(h, rope, bsz, l, layer, w_in_all, sinks, conv_w, conv_b, dt_bias, a_log, ssd_d, ssd_norm,
                  lam_re, lam_im, log_step, b_re, b_im, c_re, c_im, s5_d, w_glu, b_glu, w_branch, w_out_all):
    m = h.shape[0]
    o0 = Q_W + 2 * KV_W
    o1 = o0 + SSD_INNER
    o2 = o1 + SSD_CONV_CH
    o3 = o2 + SSD_HEADS
    cast = lambda w: w.astype(BF)
    w_in = [(w_in_all, layer)]
    tm = min(1024, m)
    tab = [pl.BlockSpec((tm, LANES), lambda j, i: (i, 0))] * 3
    qkv = _mm_wres(h, w_in, _ep_qkv, n=o0, extras=rope, extra_specs=tab, name="qkv_proj_rope")
    z = _mm_wres(h, w_in, _ep_plain, n=SSD_INNER, col0=o0, name="z_proj")
    xbc = _mm_wres(h, w_in, _ep_plain, n=SSD_CONV_CH, col0=o1, name="xbc_proj")
    dt_raw = _mm_wres(h, w_in, _ep_plain, n=LANES, col0=o2, out_dtype=F32, name="dt_proj")
    o4 = o3 + S5_WIDTH
    u = _mm_wres(h, w_in, _ep_plain, n=S5_WIDTH, col0=o3, out_dtype=F32, name="u_proj")
    gates = _mm_wres(h, w_in, _ep_sigmoid, n=w_in_all.shape[2] - o4, col0=o4, name="gate_proj")

    o_attn = _attention(qkv, sinks, bsz, l)
    o_ssd = _ssd(z, xbc, dt_raw, conv_w, conv_b, dt_bias, a_log, ssd_d, ssd_norm, bsz, l)
    ops = _s5_operators(lam_re, lam_im, log_step, b_re, b_im, c_re, c_im, l // S5_CHUNK)
    y_s5 = _s5_scan(u.reshape(bsz, l, S5_WIDTH), ops, bsz, l)
    o_s5 = _s5_glu(y_s5, u, s5_d, cast(w_glu), b_glu)

    wb = cast(w_branch)
    merged = _branch_merge(o_attn, o_ssd, o_s5, wb[:Q_W], wb[Q_W:Q_W + SSD_INNER], wb[Q_W + SSD_INNER:], gates)
    return _mm_wres(merged, [(w_out_all, layer)], _ep_plain, n=w_out_all.shape[2], name="out_proj")


def kernel(x, c, positions, w_mod, b_mod, norm_mix_pre, norm_mix_post, norm_ffn_pre, norm_ffn_post, w_in, attn_sinks, conv_w, conv_b, dt_bias, a_log, ssd_d, ssd_norm, s5_lam_re, s5_lam_im, s5_log_step, s5_b_re, s5_b_im, s5_c_re, s5_c_im, s5_d, s5_w_glu, s5_b_glu, w_branch, w_out, ffn_w1, ffn_w3, ffn_w2, w_router, moe_w1, moe_w3, moe_w2):
    bsz, l, d = x.shape
    depth = w_mod.shape[0]
    assert l % SSD_CHUNK == 0 and l % ATTN_BLOCK == 0 and l % S5_CHUNK == 0
    rope = _rope_tables(positions)
    mod = _modulation(c, w_mod, b_mod)
    cast = lambda w: w.astype(BF)

    def mods(i):
        return [mod[i, :, k * d:(k + 1) * d] for k in range(6)]

    sh1, sc1, g1, sh2, sc2, g2 = mods(0)
    h = _pre_norm(x, norm_mix_pre[0], sh1, sc1)
    for i in range(depth):
        m = _hybrid_mixer(h.reshape(bsz * l, d), rope, bsz, l, i, w_in, attn_sinks[i], conv_w[i], conv_b[i],
                          dt_bias[i], a_log[i], ssd_d[i], ssd_norm[i], s5_lam_re[i], s5_lam_im[i],
                          s5_log_step[i], s5_b_re[i], s5_b_im[i], s5_c_re[i], s5_c_im[i], s5_d[i],
                          s5_w_glu[i], s5_b_glu[i], w_branch[i], w_out)
        is_moe = i % 2 == 1
        x, h = _post_norm(m, x, g1, norm_mix_post[i], norm_ffn_pre[i], sh2, sc2, h_dtype=F32 if is_moe else BF)
        h2 = h.reshape(bsz * l, d)
        j = i // 2
        if i % 2 == 0:
            act = _mm_wres(h2, [(ffn_w1, j), (ffn_w3, j)], _ep_swiglu, n=ffn_w1.shape[2], tn=256,
                           name="swiglu_up")
            f = _matmul(act, cast(ffn_w2[j]), tm=512, name="ffn_down")
        else:
            f = _moe(h2, w_router[j], moe_w1[j], moe_w3[j], moe_w2[j])
        if i + 1 < depth:
            sh1, sc1, g1n, sh2n, sc2n, g2n = mods(i + 1)
            x, h = _post_norm(f, x, g2, norm_ffn_post[i], norm_mix_pre[i + 1], sh1, sc1)
            g1, sh2, sc2, g2 = g1n, sh2n, sc2n, g2n
        else:
            x, _ = _post_norm(f, x, g2, norm_ffn_post[i])
    return x
```

```python
import functools
import math

import jax
import jax.numpy as jnp
import numpy as np
from jax import lax
from jax.experimental import pallas as pl
from jax.experimental.pallas import tpu as pltpu

BF = jnp.bfloat16
F32 = jnp.float32

HEAD_DIM = 64
Q_HEADS = 32
KV_HEADS = 4
Q_GROUP = Q_HEADS // KV_HEADS
Q_W = Q_HEADS * HEAD_DIM
KV_W = KV_HEADS * HEAD_DIM
ATTN_BLOCK = 128
ROT_DIM = 16
ROT_HALF = ROT_DIM // 2
ROPE_THETA = 500000.0
SSD_INNER = 2048
SSD_HEADS = 32
SSD_HEAD_DIM = 64
SSD_GROUPS = 4
SSD_STATE = 128
SSD_CONV = 4
SSD_CHUNK = 128
SSD_CONV_CH = SSD_INNER + 2 * SSD_GROUPS * SSD_STATE
S5_WIDTH = 1536
S5_GROUP_SIZE = 16
S5_GROUPS = 96
S5_STATE = 64
S5_CHUNK = 16
S5_LANE_GROUPS = 8
N_EXPERTS = 8
RMS_EPS = 1e-6
LANES = 128
MXU_COLS = 256
MIB = 1024 * 1024
NEG_BIG = -1e30


def _cparams(semantics, vmem_bytes):
    limit = int(min(max(vmem_bytes * 1.25 + 4 * MIB, 16 * MIB), 60 * MIB))
    return pltpu.CompilerParams(dimension_semantics=semantics, vmem_limit_bytes=limit)


def _pick(n, candidates):
    for c in candidates:
        if n % c == 0:
            return c
    raise ValueError(f"no tile in {candidates} divides {n}")


def _silu(x):
    return x * jax.nn.sigmoid(x)


def _mod_kernel(c_ref, w_ref, b_ref, o_ref):
    c = c_ref[...]
    a = _silu(c).astype(BF)
    o_ref[0] = jnp.dot(a, w_ref[0].astype(BF), preferred_element_type=F32) + b_ref[0]


def _modulation(c, w_mod, b_mod):
    depth, d, n = w_mod.shape
    bsz = c.shape[0]
    rows = 8
    c_pad = jnp.zeros((rows, d), F32).at[:bsz].set(c)
    tn = _pick(n, (512, 256, 128))
    out = pl.pallas_call(
        _mod_kernel,
        out_shape=jax.ShapeDtypeStruct((depth, rows, n), F32),
        grid=(depth, n // tn),
        in_specs=[pl.BlockSpec((rows, d), lambda l, j: (0, 0)),
                  pl.BlockSpec((1, d, tn), lambda l, j: (l, 0, j)),
                  pl.BlockSpec((1, 1, tn), lambda l, j: (l, 0, j))],
        out_specs=pl.BlockSpec((1, rows, tn), lambda l, j: (l, 0, j)),
        compiler_params=_cparams(("parallel", "arbitrary"), 2 * d * tn * 4 + d * tn * 2),
        name="adaln_modulation",
    )(c_pad, w_mod, b_mod.reshape(depth, 1, n))
    return out[:, :bsz]


def _rms(x, w):
    return x * lax.rsqrt(jnp.mean(x * x, axis=-1, keepdims=True) + RMS_EPS) * w


def _pre_norm_kernel(x_ref, w_ref, sh_ref, sc_ref, h_ref):
    h = _rms(x_ref[0], w_ref[...])
    h_ref[0] = (h * (1.0 + sc_ref[0]) + sh_ref[0]).astype(h_ref.dtype)


def _pre_norm(x, w, shift, scale):
    bsz, l, d = x.shape
    tl = _pick(l, (256, 128))
    vec = pl.BlockSpec((1, 1, d), lambda b, i: (b, 0, 0))
    return pl.pallas_call(
        _pre_norm_kernel,
        out_shape=jax.ShapeDtypeStruct((bsz, l, d), BF),
        grid=(bsz, l // tl),
        in_specs=[pl.BlockSpec((1, tl, d), lambda b, i: (b, i, 0)),
                  pl.BlockSpec((1, d), lambda b, i: (0, 0)), vec, vec],
        out_specs=pl.BlockSpec((1, tl, d), lambda b, i: (b, i, 0)),
        compiler_params=_cparams(("parallel", "parallel"), 2 * tl * d * 6 + 2 * tl * d * 4),
        name="pre_norm",
    )(x, w.reshape(1, d), shift.reshape(bsz, 1, d), scale.reshape(bsz, 1, d))


def _post_norm_kernel(m_ref, x_ref, g_ref, wpost_ref, wpre_ref, sh_ref, sc_ref, xo_ref, h_ref):
    xn = x_ref[0] + g_ref[0] * _rms(m_ref[0].astype(F32), wpost_ref[...])
    xo_ref[0] = xn
    h = _rms(xn, wpre_ref[...])
    h_ref[0] = (h * (1.0 + sc_ref[0]) + sh_ref[0]).astype(h_ref.dtype)


def _post_norm_last_kernel(m_ref, x_ref, g_ref, wpost_ref, xo_ref):
    xo_ref[0] = x_ref[0] + g_ref[0] * _rms(m_ref[0].astype(F32), wpost_ref[...])


def _post_norm(m, x, gate, w_post, w_pre=None, shift=None, scale=None, h_dtype=BF):
    bsz, l, d = x.shape
    tl = _pick(l, (256, 128))
    row = pl.BlockSpec((1, tl, d), lambda b, i: (b, i, 0))
    vec = pl.BlockSpec((1, 1, d), lambda b, i: (b, 0, 0))
    par = pl.BlockSpec((1, d), lambda b, i: (0, 0))
    m = m.reshape(bsz, l, d)
    gate = gate.reshape(bsz, 1, d)
    cp = _cparams(("parallel", "parallel"), 2 * tl * d * (2 + 4 + 4 + 2) + 3 * tl * d * 4)
    if w_pre is None:
        return pl.pallas_call(
            _post_norm_last_kernel,
            out_shape=jax.ShapeDtypeStruct((bsz, l, d), F32),
            grid=(bsz, l // tl),
            in_specs=[row, row, vec, par],
            out_specs=row,
            compiler_params=cp,
            name="post_norm_last",
        )(m, x, gate, w_post.reshape(1, d)), None
    return pl.pallas_call(
        _post_norm_kernel,
        out_shape=(jax.ShapeDtypeStruct((bsz, l, d), F32), jax.ShapeDtypeStruct((bsz, l, d), h_dtype)),
        grid=(bsz, l // tl),
        in_specs=[row, row, vec, par, par, vec, vec],
        out_specs=(row, row),
        compiler_params=cp,
        name="post_norm",
    )(m, x, gate, w_post.reshape(1, d), w_pre.reshape(1, d), shift.reshape(bsz, 1, d),
      scale.reshape(bsz, 1, d))


def _mm_call(body, a_list, w_list, extras, extra_specs, *, n, tm, tn, out_dtype, name):
    m = a_list[0].shape[0]
    tm = min(tm, m)
    assert m % tm == 0 and n % tn == 0
    a_specs = [pl.BlockSpec((tm, a.shape[1]), lambda i, j: (i, 0)) for a in a_list]
    w_specs = [pl.BlockSpec((w.shape[0], tn), lambda i, j: (0, j)) for w in w_list]
    vmem = sum(2 * tm * a.shape[1] * a.dtype.itemsize for a in a_list)
    vmem += sum(2 * w.shape[0] * tn * w.dtype.itemsize for w in w_list)
    vmem += 2 * tm * tn * 4 * (1 + len(extras)) + 3 * tm * tn * 4
    return pl.pallas_call(
        body,
        out_shape=jax.ShapeDtypeStruct((m, n), out_dtype),
        grid=(m // tm, n // tn),
        in_specs=a_specs + w_specs + list(extra_specs(tm, tn)),
        out_specs=pl.BlockSpec((tm, tn), lambda i, j: (i, j)),
        compiler_params=_cparams(("parallel", "arbitrary"), vmem),
        name=name,
    )(*a_list, *w_list, *extras)


def _dot(a, w):
    return jnp.dot(a, w, preferred_element_type=F32)


def _plain_body(a_ref, w_ref, o_ref):
    o_ref[...] = _dot(a_ref[...], w_ref[...]).astype(o_ref.dtype)


def _matmul(a, w, *, tn=None, tm=1024, out_dtype=BF, body=_plain_body, name="matmul"):
    n = w.shape[1]
    tn = tn or _pick(n, (512, 256, 128))
    return _mm_call(body, [a], [w], [], lambda tm_, tn_: [], n=n, tm=tm, tn=tn,
                    out_dtype=out_dtype, name=name)


def _wres_kernel(n_w, shift, epilogue, a_ref, *refs):
    n_in = n_w * (2 if shift else 1)
    w_refs, nxt_refs = refs[:n_w], refs[n_w:n_in]
    extras, o_ref, wb = refs[n_in:-n_w - 1], refs[-n_w - 1], refs[-n_w:]

    @pl.when(pl.program_id(1) == 0)
    def _():
        for idx, (w_ref, b) in enumerate(zip(w_refs, wb)):
            w = w_ref[...]
            if shift:
                w = jnp.concatenate([w[:, shift:], nxt_refs[idx][...][:, :shift]], axis=1)
            b[...] = w.astype(BF)

    a = a_ref[...]
    tn = o_ref.shape[1]
    sub = MXU_COLS if tn % MXU_COLS == 0 else tn
    for c in range(0, tn, sub):
        epilogue([_dot(a, b[:, c:c + sub]) for b in wb], extras, o_ref.at[:, c:c + sub], c, tn)


def _mm_wres(a, weights, epilogue, *, n, col0=0, tn=None, extras=(), extra_specs=(), tm=1024,
             out_dtype=BF, name):
    m, k = a.shape
    tm = min(tm, m)
    shift = col0 % LANES
    base = col0 - shift
    tn = tn or next(t for t in (512, 256, 128) if n % t == 0 and base % t == 0)
    assert m % tm == 0 and n % tn == 0 and base % tn == 0
    c0, per = base // tn, tn // LANES
    w_specs = [pl.BlockSpec((k, tn), functools.partial(lambda lead, j, i: (lead, c0 + j), lead))
               for _, lead in weights]
    if shift:
        w_specs += [pl.BlockSpec((k, LANES), functools.partial(lambda lead, j, i: (lead, (c0 + j + 1) * per), lead))
                    for _, lead in weights]
    w_args = [w.reshape(-1, w.shape[-1]) for w, _ in weights] * (2 if shift else 1)
    vmem = 2 * tm * k * 2 + len(weights) * (2 * k * (tn + LANES) * 4 + 2 * k * tn * 2) + 2 * tm * tn * 4
    vmem += 4 * tm * tn * 4
    return pl.pallas_call(
        functools.partial(_wres_kernel, len(weights), shift, epilogue),
        out_shape=jax.ShapeDtypeStruct((m, n), out_dtype),
        grid=(n // tn, m // tm),
        in_specs=[pl.BlockSpec((tm, k), lambda j, i: (i, 0))] + w_specs + list(extra_specs),
        out_specs=pl.BlockSpec((tm, tn), lambda j, i: (i, j)),
        scratch_shapes=[pltpu.VMEM((k, tn), BF)] * len(weights),
        compiler_params=_cparams(("arbitrary", "arbitrary"), vmem),
        name=name,
    )(a, *w_args, *extras)


def _ep_plain(accs, extras, o_ref, col, tile_w):
    o_ref[...] = accs[0].astype(o_ref.dtype)


def _ep_sigmoid(accs, extras, o_ref, col, tile_w):
    o_ref[...] = jax.nn.sigmoid(accs[0]).astype(o_ref.dtype)


def _ep_swiglu(accs, extras, o_ref, col, tile_w):
    o_ref[...] = (_silu(accs[0]) * accs[1]).astype(o_ref.dtype)


def _ep_qkv(accs, extras, o_ref, col, tile_w):
    cos_ref, sa_ref, sb_ref = extras
    acc = accs[0]
    first = pl.program_id(0) * tile_w + col
    cols = []
    for c in range(acc.shape[1] // LANES):
        r = acc[:, c * LANES:(c + 1) * LANES]
        g0 = first + c * LANES
        roped = (r * cos_ref[...] + pltpu.roll(r, ROT_HALF, axis=1) * sa_ref[...]
                 + pltpu.roll(r, LANES - ROT_HALF, axis=1) * sb_ref[...])
        r = jnp.where(g0 < Q_W + KV_W, roped, r)
        cols.append(r * jnp.where(g0 < Q_W, HEAD_DIM ** -0.5, 1.0))
    o_ref[...] = jnp.concatenate(cols, axis=1).astype(o_ref.dtype)


def _merge_body(oa_ref, ob_ref, oc_ref, wa_ref, wb_ref, wc_ref, ga_ref, gb_ref, gc_ref, o_ref):
    acc = ga_ref[...].astype(F32) * _dot(oa_ref[...], wa_ref[...])
    acc += gb_ref[...].astype(F32) * _dot(ob_ref[...], wb_ref[...])
    acc += gc_ref[...].astype(F32) * _dot(oc_ref[...], wc_ref[...])
    o_ref[...] = acc.astype(o_ref.dtype)


def _branch_merge(o_attn, o_ssd, o_s5, wa, wb, wc, gates):
    d = wa.shape[1]
    tn = _pick(d, (512, 256, 128))
    nt = d // tn

    def gate_specs(tm_, tn_):
        return [pl.BlockSpec((tm_, tn_), functools.partial(lambda br, i, j: (i, br * nt + j), br))
                for br in range(3)]

    return _mm_call(_merge_body, [o_attn, o_ssd, o_s5], [wa, wb, wc], [gates, gates, gates],
                    gate_specs, n=d, tm=1024, tn=tn, out_dtype=BF, name="branch_merge")


def _attn_kernel(sink_ref, q_ref, kc_ref, kp_ref, vc_ref, vp_ref, o_ref):
    n = pl.program_id(1)
    qi = lax.broadcasted_iota(jnp.int32, (ATTN_BLOCK, 2 * ATTN_BLOCK), 0)
    sj = lax.broadcasted_iota(jnp.int32, (ATTN_BLOCK, 2 * ATTN_BLOCK), 1)
    rel = qi + ATTN_BLOCK - sj
    valid = (rel >= 0) & (rel < ATTN_BLOCK) & ((sj >= ATTN_BLOCK) | (n > 0))
    q = q_ref[0]
    kv = 2 * ATTN_BLOCK
    pair_w = 2 * HEAD_DIM
    lane = lax.broadcasted_iota(jnp.int32, (ATTN_BLOCK, pair_w), 1)
    first = lane < HEAD_DIM
    zeros = jnp.zeros((kv, HEAD_DIM), BF)
    ones = jnp.ones((kv, HEAD_DIM), BF)
    outs = []
    for g in range(KV_HEADS):
        sl = slice(g * HEAD_DIM, (g + 1) * HEAD_DIM)
        kbt = jnp.concatenate([kp_ref[0, g], kc_ref[0, g]], axis=1)
        vb = jnp.concatenate([vp_ref[0][:, sl], vc_ref[0][:, sl]], axis=0)
        pv = jnp.concatenate([jnp.concatenate([vb, zeros, ones, zeros], axis=1),
                              jnp.concatenate([zeros, vb, zeros, ones], axis=1)], axis=0)
        for r in range(0, Q_GROUP, 2):
            ps, sinks = [], []
            for k in range(2):
                hd = g * Q_GROUP + r + k
                s = _dot(q[:, hd * HEAD_DIM:(hd + 1) * HEAD_DIM], kbt)
                s = jnp.where(valid, s, NEG_BIG)
                mx = jnp.maximum(jnp.max(s, axis=-1, keepdims=True), sink_ref[hd])
                ps.append(jnp.exp(s - mx).astype(BF))
                sinks.append(jnp.exp(sink_ref[hd] - mx))
            res = _dot(jnp.concatenate(ps, axis=1), pv)
            den = res[:, pair_w:] + jnp.where(first, sinks[0], sinks[1])
            outs.append(res[:, :pair_w] / den)
    o_ref[0] = jnp.concatenate(outs, axis=1).astype(o_ref.dtype)


def _attention(qkv, sinks, bsz, l):
    nb = l // ATTN_BLOCK
    qkv = qkv.reshape(bsz, l, Q_W + 2 * KV_W)
    k_t = qkv[:, :, Q_W:Q_W + KV_W].reshape(bsz, l, KV_HEADS, HEAD_DIM).transpose(0, 2, 3, 1)
    vcol = Q_W // KV_W + 1
    kspec = lambda idx: pl.BlockSpec((1, KV_HEADS, HEAD_DIM, ATTN_BLOCK), lambda b, n: (b, 0, 0, idx(n)))
    vspec = lambda idx: pl.BlockSpec((1, ATTN_BLOCK, KV_W), lambda b, n: (b, idx(n), vcol))
    cur = lambda n: n
    prev = lambda n: jnp.maximum(n - 1, 0)
    out = pl.pallas_call(
        _attn_kernel,
        out_shape=jax.ShapeDtypeStruct((bsz, l, Q_W), BF),
        grid=(bsz, nb),
        in_specs=[pl.BlockSpec(memory_space=pltpu.SMEM),
                  pl.BlockSpec((1, ATTN_BLOCK, Q_W), lambda b, n: (b, n, 0)),
                  kspec(cur), kspec(prev), vspec(cur), vspec(prev)],
        out_specs=pl.BlockSpec((1, ATTN_BLOCK, Q_W), lambda b, n: (b, n, 0)),
        compiler_params=_cparams(("parallel", "parallel"), 8 * MIB),
        name="sliding_window_attention",
    )(sinks.astype(F32), qkv, k_t, k_t, qkv, qkv)
    return out.reshape(bsz * l, Q_W)


def _split_dot(f, e):
    hi = f.astype(BF)
    lo = (f - hi.astype(F32)).astype(BF)
    return _dot(hi, e) + _dot(lo, e)


def _ssd_kernel(z_ref, xbc_ref, dt_ref, cw_ref, cb_ref, dtb_ref, alog_ref, dsk_ref, nw_ref,
                o_ref, ext_ref, state_ref):
    c = pl.program_id(1)
    q = SSD_CHUNK
    halo = 8

    @pl.when(c == 0)
    def _():
        ext_ref[0:halo, :] = jnp.zeros((halo, SSD_CONV_CH), F32)
        state_ref[...] = jnp.zeros_like(state_ref)

    ext_ref[halo:halo + q, :] = xbc_ref[0].astype(F32)
    conv = cb_ref[...] + cw_ref[SSD_CONV - 1:SSD_CONV, :] * ext_ref[halo:halo + q, :]
    for j in range(SSD_CONV - 1):
        off = halo - (SSD_CONV - 1) + j
        conv = conv + cw_ref[j:j + 1, :] * ext_ref[off:off + q, :]
    tail = ext_ref[q:q + halo, :]
    ext_ref[0:halo, :] = tail
    xbc = _silu(conv)
    xs = xbc[:, :SSD_INNER]
    xs_bf = xs.astype(BF)
    bm = xbc[:, SSD_INNER:SSD_INNER + SSD_GROUPS * SSD_STATE].astype(BF)
    cm = xbc[:, SSD_INNER + SSD_GROUPS * SSD_STATE:].astype(BF)

    dt = jax.nn.softplus(dt_ref[0] + dtb_ref[...])
    a = -jnp.exp(alog_ref[...])
    cs = dt * a
    row = lax.broadcasted_iota(jnp.int32, (q, LANES), 0)
    sh = 1
    while sh < q:
        cs = cs + jnp.where(row >= sh, pltpu.roll(cs, sh, axis=0), 0.0)
        sh *= 2
    cs_last = cs[q - 1:q, :]
    ecs = jnp.exp(cs)
    dte = jnp.exp(cs_last - cs) * dt
    cs_t = cs.T
    dt_t = dt.T

    hl = lax.broadcasted_iota(jnp.int32, (LANES, SSD_INNER), 0)
    hc = lax.broadcasted_iota(jnp.int32, (LANES, SSD_INNER), 1)
    expand = jnp.where(hl == hc // SSD_HEAD_DIM, 1.0, 0.0).astype(BF)
    ecs_x = _split_dot(ecs, expand)
    dte_x = _split_dot(dte, expand)

    li = lax.broadcasted_iota(jnp.int32, (q, q), 0)
    si = lax.broadcasted_iota(jnp.int32, (q, q), 1)
    causal = li >= si
    heads_per_group = SSD_HEADS // SSD_GROUPS
    gw = heads_per_group * SSD_HEAD_DIM
    y_parts = []
    for g in range(SSD_GROUPS):
        ns = slice(g * SSD_STATE, (g + 1) * SSD_STATE)
        gs = slice(g * gw, (g + 1) * gw)
        bm_g, cm_g = bm[:, ns], cm[:, ns]
        cb = lax.dot_general(cm_g, bm_g, (((1,), (1,)), ((), ())), preferred_element_type=F32)
        for r in range(heads_per_group):
            hd = g * heads_per_group + r
            seg = cs[:, hd:hd + 1] - cs_t[hd:hd + 1, :]
            decay = jnp.exp(jnp.where(causal, seg, NEG_BIG))
            w = (cb * decay * dt_t[hd:hd + 1, :]).astype(BF)
            y_parts.append(_dot(w, xs_bf[:, hd * SSD_HEAD_DIM:(hd + 1) * SSD_HEAD_DIM]))
    y = jnp.concatenate(y_parts, axis=1)
    xw = (xs * dte_x).astype(BF)
    for g in range(SSD_GROUPS):
        ns = slice(g * SSD_STATE, (g + 1) * SSD_STATE)
        gs = slice(g * gw, (g + 1) * gw)
        prev = state_ref[:, gs]
        y_off = _dot(cm[:, ns], prev.astype(BF)) * ecs_x[:, gs]
        upd = lax.dot_general(bm[:, ns], xw[:, gs], (((0,), (0,)), ((), ())),
                              preferred_element_type=F32)
        state_ref[:, gs] = prev * ecs_x[q - 1:q, gs] + upd
        yg = y[:, gs] + y_off + dsk_ref[:, gs] * xs[:, gs]
        yg = yg * _silu(z_ref[0, :, gs].astype(F32))
        yg = yg * lax.rsqrt(jnp.mean(yg * yg, axis=-1, keepdims=True) + RMS_EPS)
        o_ref[0, :, gs] = (yg * nw_ref[:, gs]).astype(o_ref.dtype)


def _ssd(z, xbc, dt_raw, conv_w, conv_b, dt_bias, a_log, d_skip, norm_w, bsz, l):
    q = SSD_CHUNK
    pad = LANES - SSD_HEADS
    row = lambda w: pl.BlockSpec((1, q, w), lambda b, c: (b, c, 0))
    par = lambda r, w: pl.BlockSpec((r, w), lambda b, c: (0, 0))
    out = pl.pallas_call(
        _ssd_kernel,
        out_shape=jax.ShapeDtypeStruct((bsz, l, SSD_INNER), BF),
        grid=(bsz, l // q),
        in_specs=[row(SSD_INNER), row(SSD_CONV_CH), row(LANES),
                  par(SSD_CONV, SSD_CONV_CH), par(1, SSD_CONV_CH), par(1, LANES), par(1, LANES),
                  par(1, SSD_INNER), par(1, SSD_INNER)],
        out_specs=row(SSD_INNER),
        scratch_shapes=[pltpu.VMEM((q + 8, SSD_CONV_CH), F32), pltpu.VMEM((SSD_STATE, SSD_INNER), F32)],
        compiler_params=_cparams(("parallel", "arbitrary"), 24 * MIB),
        name="ssd_chunked",
    )(z.reshape(bsz, l, SSD_INNER), xbc.reshape(bsz, l, SSD_CONV_CH), dt_raw.reshape(bsz, l, LANES),
      conv_w, conv_b.reshape(1, -1), jnp.pad(dt_bias, (0, pad)).reshape(1, LANES),
      jnp.pad(a_log, (0, pad)).reshape(1, LANES),
      jnp.repeat(d_skip, SSD_HEAD_DIM).reshape(1, SSD_INNER), norm_w.reshape(1, SSD_INNER))
    return out.reshape(bsz * l, SSD_INNER)


def _s5_operators(lam_re, lam_im, log_step, b_re, b_im, c_re, c_im, n_chunks):
    hp = lax.Precision.HIGHEST
    qn = S5_CHUNK
    lr, li = lam_re.astype(F32), lam_im.astype(F32)
    step = jnp.exp(log_step.astype(F32))[:, None]
    mag = jnp.exp(lr * step)
    a_re, a_im = mag * jnp.cos(li * step), mag * jnp.sin(li * step)
    den = lr * lr + li * li
    coef_re = ((a_re - 1.0) * lr + a_im * li) / den
    coef_im = (a_im * lr - (a_re - 1.0) * li) / den
    br, bi = b_re.astype(F32), b_im.astype(F32)
    bb_re = coef_re[..., None] * br - coef_im[..., None] * bi
    bb_im = coef_re[..., None] * bi + coef_im[..., None] * br

    def power(d):
        d = d.astype(F32)[None, :, None]
        m = jnp.exp(lr[:, None, :] * step[:, None, :] * d)
        ang = li[:, None, :] * step[:, None, :] * d
        return m * jnp.cos(ang), m * jnp.sin(ang)

    p_re, p_im = power(jnp.arange(qn + 1))
    cr, ci = c_re.astype(F32), c_im.astype(F32)
    m_re = cr[:, None] * p_re[:, :, None, :] - ci[:, None] * p_im[:, :, None, :]
    m_im = cr[:, None] * p_im[:, :, None, :] + ci[:, None] * p_re[:, :, None, :]
    kern = (jnp.einsum('gdip,gpj->gdij', m_re, bb_re, precision=hp)
            - jnp.einsum('gdip,gpj->gdij', m_im, bb_im, precision=hp))
    gl, nlg, gs, ps = S5_LANE_GROUPS, S5_GROUPS // S5_LANE_GROUPS, S5_GROUP_SIZE, S5_STATE
    lag_op = kern[:, :qn].reshape(nlg, gl, qn, gs, gs).transpose(0, 1, 4, 2, 3)
    lag_op = lag_op.reshape(nlg, LANES, qn * gs)
    e_re, e_im = p_re[:, qn - 1 - jnp.arange(qn)], p_im[:, qn - 1 - jnp.arange(qn)]
    bt_re, bt_im = bb_re.transpose(0, 2, 1)[:, None], bb_im.transpose(0, 2, 1)[:, None]
    be = jnp.stack([e_re[:, :, None, :] * bt_re - e_im[:, :, None, :] * bt_im,
                    e_re[:, :, None, :] * bt_im + e_im[:, :, None, :] * bt_re], axis=3)
    b_end = be.reshape(nlg, gl, qn, gs, 2 * ps).transpose(0, 2, 1, 3, 4)
    b_end = b_end.reshape(nlg, qn * LANES, 2 * ps)
    ci = jnp.stack([m_re[:, 1:].transpose(0, 3, 1, 2), -m_im[:, 1:].transpose(0, 3, 1, 2)], axis=1)
    c_in = ci.reshape(nlg, gl, 2, ps, qn * gs).transpose(0, 2, 1, 3, 4)
    c_in = c_in.reshape(nlg, 2 * gl * ps, qn * gs)
    n_steps = max(1, (n_chunks - 1).bit_length())
    s_re, s_im = power(qn * (2 ** jnp.arange(n_steps)))
    lanes = lambda t: t.reshape(nlg, gl, n_steps, ps).transpose(0, 2, 1, 3).reshape(nlg, n_steps, gl * ps)
    pw1 = jnp.concatenate([lanes(s_re), lanes(s_re)], axis=-1)
    pw2 = jnp.concatenate([-lanes(s_im), lanes(s_im)], axis=-1)
    return lag_op.astype(BF), b_end.astype(BF), c_in.astype(BF), pw1, pw2


def _s5_kernel(u_ref, lag_ref, bend_ref, cin_ref, t_out_ref, t_state_ref, pw1_ref, pw2_ref, y_ref,
               toep_ref, bend_full, cin_full, ucat_ref):
    qn, gs, ps = S5_CHUNK, S5_GROUP_SIZE, S5_STATE
    n_chunks = u_ref.shape[1] // qn
    n_steps = pw1_ref.shape[1]
    st = pw1_ref.shape[2]
    half = st // 2

    @pl.when(pl.program_id(1) == 0)
    def _():
        blk = lambda d: slice(d * LANES, (d + 1) * LANES)
        col = lambda n: lax.broadcasted_iota(jnp.int32, (n, 1), 0)
        out_group = lax.broadcasted_iota(jnp.int32, (1, LANES), 1) // gs
        lag_diag = col(LANES) // gs == out_group
        cin_diag = (col(st) % half) // ps == out_group
        toep_ref[...] = jnp.zeros_like(toep_ref)
        for d in range(qn):
            lag_d = jnp.where(lag_diag, _dot(lag_ref[0], t_out_ref[:, blk(d)]), 0.0).astype(BF)
            for s in range(qn - d):
                toep_ref[blk(s), blk(s + d)] = lag_d
            cin_full[:, blk(d)] = jnp.where(cin_diag, _dot(cin_ref[0], t_out_ref[:, blk(d)]), 0.0).astype(BF)
        in_group = (col(qn * LANES) % LANES) // gs
        for k in range(st // LANES):
            state_group = ((k * LANES + lax.broadcasted_iota(jnp.int32, (1, LANES), 1)) % half) // ps
            bend_full[:, blk(k)] = jnp.where(in_group == state_group,
                                             _dot(bend_ref[0], t_state_ref[:, blk(k)]), 0.0).astype(BF)

    for s in range(qn):
        ucat_ref[:, s * LANES:(s + 1) * LANES] = u_ref[0, pl.ds(s, n_chunks, stride=qn), :].astype(BF)
    ucat = ucat_ref[...]
    x = _dot(ucat, bend_full[...])
    row = lax.broadcasted_iota(jnp.int32, x.shape, 0)
    for k in range(n_steps):
        sh = 1 << k
        xs = jnp.where(row >= sh, pltpu.roll(x, sh, axis=0), 0.0)
        x = x + pw1_ref[0, k:k + 1, :] * xs + pw2_ref[0, k:k + 1, :] * pltpu.roll(xs, half, axis=1)
    x_in = jnp.where(row >= 1, pltpu.roll(x, 1, axis=0), 0.0)
    y = _dot(ucat, toep_ref[...]) + _dot(x_in.astype(BF), cin_full[...])
    for t in range(qn):
        y_ref[0, pl.ds(t, n_chunks, stride=qn), :] = y[:, t * LANES:(t + 1) * LANES]


def _s5_scan(u, ops, bsz, l):
    lag_op, b_end, c_in, pw1, pw2 = ops
    qn, gs, ps = S5_CHUNK, S5_GROUP_SIZE, S5_STATE
    nlg = S5_WIDTH // LANES
    nc = l // qn
    wide = qn * LANES
    st = pw1.shape[2]
    r, c = np.arange(qn * gs)[:, None], np.arange(wide)[None, :]
    t_out = jnp.asarray((r // gs == c // LANES) & (r % gs == c % gs), BF)
    r, c = np.arange(2 * ps)[:, None], np.arange(st)[None, :]
    t_state = jnp.asarray((r // ps == c // (st // 2)) & (r % ps == c % ps), BF)
    wspec = lambda rows, cols: pl.BlockSpec((1, rows, cols), lambda g, b: (g, 0, 0))
    full = lambda a: pl.BlockSpec(a.shape, lambda g, b: (0, 0))
    seq = pl.BlockSpec((1, l, LANES), lambda g, b: (b, 0, g))
    y = pl.pallas_call(
        _s5_kernel,
        out_shape=jax.ShapeDtypeStruct((bsz, l, S5_WIDTH), F32),
        grid=(nlg, bsz),
        in_specs=[seq, wspec(LANES, qn * gs), wspec(wide, 2 * ps), wspec(st, qn * gs), full(t_out), full(t_state),
                  wspec(pw1.shape[1], st), wspec(pw1.shape[1], st)],
        out_specs=seq,
        scratch_shapes=[pltpu.VMEM((wide, wide), BF), pltpu.VMEM((wide, st), BF), pltpu.VMEM((st, wide), BF),
                        pltpu.VMEM((nc, wide), BF)],
        compiler_params=_cparams(("arbitrary", "arbitrary"),
                                 4 * l * LANES * 4 + wide * wide * 2 + 2 * wide * st * 2 + 4 * wide * LANES * 2
                                 + nc * wide * 2 + 3 * nc * wide * 4 + 4 * nc * st * 4),
        name="s5_chunked_scan",
    )(u, lag_op, b_end, c_in, t_out, t_state, pw1, pw2)
    return y.reshape(bsz * l, S5_WIDTH)


def _s5_glu_kernel(y_ref, u_ref, d_ref, w_ref, b_ref, o_ref):
    y = y_ref[...].astype(F32) + d_ref[...] * u_ref[...].astype(F32)
    v = jax.nn.gelu(y, approximate=True)
    o_ref[...] = (v * jax.nn.sigmoid(_dot(v.astype(BF), w_ref[...]) + b_ref[...])).astype(o_ref.dtype)


def _s5_glu(y, u, d_skip, w_glu, b_glu):
    m, w = y.shape
    tm = _pick(m, (512, 256, 128))
    row = pl.BlockSpec((tm, w), lambda i: (i, 0))
    par = pl.BlockSpec((1, w), lambda i: (0, 0))
    return pl.pallas_call(
        _s5_glu_kernel,
        out_shape=jax.ShapeDtypeStruct((m, w), BF),
        grid=(m // tm,),
        in_specs=[row, row, par, pl.BlockSpec((w, w), lambda i: (0, 0)), par],
        out_specs=row,
        compiler_params=_cparams(("parallel",), 4 * tm * w * 4 + 2 * tm * w * 2 + 2 * w * w * 2 + 4 * tm * w * 4),
        name="s5_gelu_glu",
    )(y, u, d_skip.reshape(1, w), w_glu, b_glu.reshape(1, w))


MOE_ROW_TILE = 256
INFO_E1, INFO_E2, INFO_W1, INFO_W2, INFO_R1, INFO_R2 = range(6)


def _split_bf16(x):
    hi = x.astype(BF)
    return hi, (x - hi.astype(F32)).astype(BF)


def _router_kernel(h_ref, w_ref, info_ref, cnt_ref, carry_ref):
    i = pl.program_id(0)

    @pl.when(i == 0)
    def _():
        carry_ref[...] = jnp.zeros_like(carry_ref)

    hh, hl = _split_bf16(h_ref[...])
    wh, wl = _split_bf16(w_ref[...])
    logits = _dot(hh, wh) + _dot(hl, wh) + _dot(hh, wl)
    tm = logits.shape[0]
    lane = lax.broadcasted_iota(jnp.int32, logits.shape, 1)
    lg = jnp.where(lane < N_EXPERTS, logits, -jnp.inf)
    m1 = jnp.max(lg, axis=-1, keepdims=True)
    i1 = jnp.min(jnp.where(lg == m1, lane, LANES), axis=-1, keepdims=True)
    lg2 = jnp.where(lane == i1, -jnp.inf, lg)
    m2 = jnp.max(lg2, axis=-1, keepdims=True)
    i2 = jnp.min(jnp.where(lg2 == m2, lane, LANES), axis=-1, keepdims=True)
    e = jnp.exp(m2 - m1)
    w1 = 1.0 / (1.0 + e)
    w2 = e * w1
    onehot = jnp.where(lane == i1, 1.0, 0.0) + jnp.where(lane == i2, 1.0, 0.0)
    rt = lax.broadcasted_iota(jnp.int32, (tm, tm), 0)
    ct = lax.broadcasted_iota(jnp.int32, (tm, tm), 1)
    before = jnp.where(rt > ct, 1.0, 0.0).astype(BF)
    rank = _dot(before, onehot.astype(BF)) + carry_ref[...]
    r1 = jnp.sum(jnp.where(lane == i1, rank, 0.0), axis=-1, keepdims=True)
    r2 = jnp.sum(jnp.where(lane == i2, rank, 0.0), axis=-1, keepdims=True)
    carry_ref[...] += jnp.sum(onehot, axis=0, keepdims=True)
    cnt_ref[...] = carry_ref[...]
    info = jnp.zeros_like(logits)
    for slot, val in ((INFO_E1, i1.astype(F32)), (INFO_E2, i2.astype(F32)), (INFO_W1, w1), (INFO_W2, w2),
                      (INFO_R1, r1), (INFO_R2, r2)):
        info = jnp.where(lane == slot, val, info)
    info_ref[...] = info


def _router(h, w_router):
    m, d = h.shape
    tm = _pick(m, (512, 256, 128))
    w_pad = jnp.zeros((d, LANES), F32).at[:, :N_EXPERTS].set(w_router.astype(F32))
    return pl.pallas_call(
        _router_kernel,
        out_shape=(jax.ShapeDtypeStruct((m, LANES), F32), jax.ShapeDtypeStruct((1, LANES), F32)),
        grid=(m // tm,),
        in_specs=[pl.BlockSpec((tm, d), lambda i: (i, 0)), pl.BlockSpec((d, LANES), lambda i: (0, 0))],
        out_specs=(pl.BlockSpec((tm, LANES), lambda i: (i, 0)), pl.BlockSpec((1, LANES), lambda i: (0, 0))),
        scratch_shapes=[pltpu.VMEM((1, LANES), F32)],
        compiler_params=_cparams(("arbitrary",), 2 * tm * d * 4 + 2 * tm * d * 2 + 2 * d * LANES * 4
                                 + 2 * tm * tm * 2 + 8 * tm * LANES * 4),
        name="moe_router",
    )(h, w_pad)


def _row_copy(src_hbm, src_row, buf, slot, dst_row, sem):
    return pltpu.make_async_copy(src_hbm.at[pl.ds(src_row, 1)], buf.at[slot, pl.ds(dst_row, 1)], sem.at[slot])


def _dispatch_kernel(tok_ref, nu_ref, h_hbm, o_ref, buf, sem):
    i = pl.program_id(0)
    rows = o_ref.shape[0]
    nu = nu_ref[0]

    def issue(tile, slot):
        def body(r, carry):
            _row_copy(h_hbm, tok_ref[tile * rows + r], buf, slot, r, sem).start()
            return carry
        lax.fori_loop(0, rows, body, 0, unroll=8)

    def wait(slot):
        def body(r, carry):
            _row_copy(h_hbm, 0, buf, slot, 0, sem).wait()
            return carry
        lax.fori_loop(0, rows, body, 0, unroll=8)

    @pl.when(i == 0)
    def _():
        issue(0, 0)

    @pl.when(i + 1 < nu)
    def _():
        issue(i + 1, (i + 1) % 2)

    @pl.when(i < nu)
    def _():
        wait(i % 2)
        o_ref[...] = buf[i % 2].astype(o_ref.dtype)

    @pl.when(i >= nu)
    def _():
        o_ref[...] = jnp.zeros_like(o_ref)


def _dispatch(h, tok_of_row, n_used, n_rows):
    d = h.shape[1]
    rt = MOE_ROW_TILE
    return pl.pallas_call(
        _dispatch_kernel,
        out_shape=jax.ShapeDtypeStruct((n_rows, d), BF),
        grid_spec=pltpu.PrefetchScalarGridSpec(
            num_scalar_prefetch=2,
            grid=(n_rows // rt,),
            in_specs=[pl.BlockSpec(memory_space=pl.ANY)],
            out_specs=pl.BlockSpec((rt, d), lambda i, tok, nu: (i, 0)),
            scratch_shapes=[pltpu.VMEM((2, rt, d), F32), pltpu.SemaphoreType.DMA((2,))]),
        compiler_params=_cparams(("arbitrary",), 2 * rt * d * 4 + 2 * rt * d * 2 + rt * d * 4),
        name="moe_dispatch",
    )(tok_of_row, n_used, h)


def _combine_kernel(p1_ref, p2_ref, y_hbm, o_ref, buf, sem):
    i = pl.program_id(0)
    n = pl.num_programs(0)
    rows = o_ref.shape[0]

    def issue(tile, slot):
        def body(r, carry):
            t = tile * rows + r
            _row_copy(y_hbm, p1_ref[t], buf, slot, r, sem).start()
            _row_copy(y_hbm, p2_ref[t], buf, slot, rows + r, sem).start()
            return carry
        lax.fori_loop(0, rows, body, 0, unroll=4)

    def wait(slot):
        def body(r, carry):
            _row_copy(y_hbm, 0, buf, slot, 0, sem).wait()
            return carry
        lax.fori_loop(0, 2 * rows, body, 0, unroll=8)

    @pl.when(i == 0)
    def _():
        issue(0, 0)

    @pl.when(i + 1 < n)
    def _():
        issue(i + 1, (i + 1) % 2)

    wait(i % 2)
    both = buf[i % 2]
    o_ref[...] = (both[:rows] + both[rows:]).astype(o_ref.dtype)


def _combine(y_rows, pos1, pos2, n_tokens):
    d = y_rows.shape[1]
    tt = 128
    return pl.pallas_call(
        _combine_kernel,
        out_shape=jax.ShapeDtypeStruct((n_tokens, d), BF),
        grid_spec=pltpu.PrefetchScalarGridSpec(
            num_scalar_prefetch=2,
            grid=(n_tokens // tt,),
            in_specs=[pl.BlockSpec(memory_space=pl.ANY)],
            out_specs=pl.BlockSpec((tt, d), lambda i, p1, p2: (i, 0)),
            scratch_shapes=[pltpu.VMEM((2, 2 * tt, d), F32), pltpu.SemaphoreType.DMA((2,))]),
        compiler_params=_cparams(("arbitrary",), 4 * tt * d * 4 + 2 * tt * d * 2 + 2 * tt * d * 4),
        name="moe_combine",
    )(pos1, pos2, y_rows)


def _expert_changed(te_ref, i):
    return jnp.logical_or(i == 0, te_ref[i] != te_ref[jnp.maximum(i - 1, 0)])


def _moe_up_kernel(te_ref, nu_ref, a_ref, w1_ref, w3_ref, rw_ref, o_ref, w1b, w3b):
    i = pl.program_id(1)
    valid = i < nu_ref[0]

    @pl.when(jnp.logical_and(valid, _expert_changed(te_ref, i)))
    def _():
        w1b[...] = w1_ref[0].astype(BF)
        w3b[...] = w3_ref[0].astype(BF)

    @pl.when(valid)
    def _():
        a = a_ref[...]
        tn = o_ref.shape[1]
        sub = MXU_COLS if tn % MXU_COLS == 0 else tn
        for c in range(0, tn, sub):
            act = _silu(_dot(a, w1b[:, c:c + sub])) * _dot(a, w3b[:, c:c + sub])
            o_ref[:, c:c + sub] = (act * rw_ref[...]).astype(o_ref.dtype)

    @pl.when(jnp.logical_not(valid))
    def _():
        o_ref[...] = jnp.zeros_like(o_ref)


def _moe_down_kernel(te_ref, nu_ref, a_ref, w_ref, o_ref, wb):
    i = pl.program_id(1)
    valid = i < nu_ref[0]

    @pl.when(jnp.logical_and(valid, _expert_changed(te_ref, i)))
    def _():
        wb[...] = w_ref[0].astype(BF)

    @pl.when(valid)
    def _():
        o_ref[...] = _dot(a_ref[...], wb[...]).astype(o_ref.dtype)

    @pl.when(jnp.logical_not(valid))
    def _():
        o_ref[...] = jnp.zeros_like(o_ref)


def _grouped_call(kern, rows_in, weights, extras, *, tn, out_dtype, tile_expert, n_used, name):
    n_rows, k = rows_in.shape
    n = weights[0].shape[2]
    rt = MOE_ROW_TILE
    last = lambda i, nu: jnp.minimum(i, nu[0] - 1)
    row_spec = lambda w: pl.BlockSpec((rt, w), lambda j, i, te, nu: (last(i, nu), 0))
    w_spec = pl.BlockSpec((1, k, tn), lambda j, i, te, nu: (te[last(i, nu)], 0, j))
    vmem = 2 * rt * k * 2 + len(weights) * (2 * k * tn * 4 + k * tn * 2) + 2 * rt * tn * 4 + 4 * rt * tn * 4
    return pl.pallas_call(
        kern,
        out_shape=jax.ShapeDtypeStruct((n_rows, n), out_dtype),
        grid_spec=pltpu.PrefetchScalarGridSpec(
            num_scalar_prefetch=2,
            grid=(n // tn, n_rows // rt),
            in_specs=[row_spec(k)] + [w_spec] * len(weights) + [row_spec(e.shape[1]) for e in extras],
            out_specs=pl.BlockSpec((rt, tn), lambda j, i, te, nu: (i, j)),
            scratch_shapes=[pltpu.VMEM((k, tn), BF)] * len(weights)),
        compiler_params=_cparams(("arbitrary", "arbitrary"), vmem),
        name=name,
    )(tile_expert, n_used, rows_in, *weights, *extras)


def _moe(h, w_router, w1, w3, w2):
    m, d = h.shape
    ne, _, f = w1.shape
    rt = MOE_ROW_TILE
    n_rows = 2 * m + ne * rt
    info, counts = _router(h, w_router)
    e1, e2 = info[:, INFO_E1].astype(jnp.int32), info[:, INFO_E2].astype(jnp.int32)
    r1, r2 = info[:, INFO_R1].astype(jnp.int32), info[:, INFO_R2].astype(jnp.int32)
    tiles = (counts[0, :ne].astype(jnp.int32) + rt - 1) // rt
    cum_tiles = jnp.cumsum(tiles)
    base = (cum_tiles - tiles) * rt
    pos1, pos2 = base[e1] + r1, base[e2] + r2
    tok = jnp.arange(m, dtype=F32)
    assert m < 2 ** 24
    per_token = jnp.concatenate([jnp.stack([tok, info[:, INFO_W1]], axis=-1),
                                 jnp.stack([tok, info[:, INFO_W2]], axis=-1)])
    per_row = jnp.zeros((n_rows, 2), F32).at[jnp.concatenate([pos1, pos2])].set(per_token)
    tok_of_row, row_w = per_row[:, 0].astype(jnp.int32), per_row[:, 1]
    n_used = cum_tiles[-1:].astype(jnp.int32)
    tile_expert = jnp.minimum(jnp.searchsorted(cum_tiles, jnp.arange(n_rows // rt, dtype=jnp.int32),
                                               side='right'), ne - 1).astype(jnp.int32)
    rows = _dispatch(h, tok_of_row, n_used, n_rows)
    grouped = functools.partial(_grouped_call, tile_expert=tile_expert, n_used=n_used)
    act = grouped(_moe_up_kernel, rows, [w1, w3], [row_w.reshape(n_rows, 1)],
                  tn=_pick(f, (512, 256, 128)), out_dtype=BF, name="moe_up")
    y_rows = grouped(_moe_down_kernel, act, [w2], [], tn=_pick(d, (1024, 512, 256, 128)), out_dtype=F32,
                     name="moe_down")
    return _combine(y_rows, pos1, pos2, m)


def _rope_tables(positions):
    inv_freq = ROPE_THETA ** (-jnp.arange(0, ROT_DIM, 2, dtype=F32) / ROT_DIM)
    ang = positions.astype(F32)[..., None] * inv_freq
    cos, sin = jnp.cos(ang), jnp.sin(ang)
    rest = HEAD_DIM - ROT_DIM
    ones = jnp.ones(ang.shape[:-1] + (rest,), F32)
    zeros = jnp.zeros(ang.shape[:-1] + (rest,), F32)
    z8 = jnp.zeros_like(sin)
    reps = LANES // HEAD_DIM
    flat = lambda t: jnp.tile(t, (1, 1, reps)).reshape(-1, LANES)
    cos_t = flat(jnp.concatenate([cos, cos, ones], axis=-1))
    sa_t = flat(jnp.concatenate([z8, sin, zeros], axis=-1))
    sb_t = flat(jnp.concatenate([-sin, z8, zeros], axis=-1))
    return cos_t, sa_t, sb_t


def _hybrid_mixer(h, rope, bsz, l, layer, w_in_all, sinks, conv_w, conv_b, dt_bias, a_log, ssd_d, ssd_norm,
                  lam_re, lam_im, log_step, b_re, b_im, c_re, c_im, s5_d, w_glu, b_glu, w_branch, w_out_all):
    m = h.shape[0]
    o0 = Q_W + 2 * KV_W
    o1 = o0 + SSD_INNER
    o2 = o1 + SSD_CONV_CH
    o3 = o2 + SSD_HEADS
    cast = lambda w: w.astype(BF)
    w_in = [(w_in_all, layer)]
    tm = min(1024, m)
    tab = [pl.BlockSpec((tm, LANES), lambda j, i: (i, 0))] * 3
    qkv = _mm_wres(h, w_in, _ep_qkv, n=o0, extras=rope, extra_specs=tab, name="qkv_proj_rope")
    z = _mm_wres(h, w_in, _ep_plain, n=SSD_INNER, col0=o0, name="z_proj")
    xbc = _mm_wres(h, w_in, _ep_plain, n=SSD_CONV_CH, col0=o1, name="xbc_proj")
    dt_raw = _mm_wres(h, w_in, _ep_plain, n=LANES, col0=o2, out_dtype=F32, name="dt_proj")
    o4 = o3 + S5_WIDTH
    u = _mm_wres(h, w_in, _ep_plain, n=S5_WIDTH, col0=o3, out_dtype=F32, name="u_proj")
    gates = _mm_wres(h, w_in, _ep_sigmoid, n=w_in_all.shape[2] - o4, col0=o4, name="gate_proj")

    o_attn = _attention(qkv, sinks, bsz, l)
    o_ssd = _ssd(z, xbc, dt_raw, conv_w, conv_b, dt_bias, a_log, ssd_d, ssd_norm, bsz, l)
    ops = _s5_operators(lam_re, lam_im, log_step, b_re, b_im, c_re, c_im, l // S5_CHUNK)
    y_s5 = _s5_scan(u.reshape(bsz, l, S5_WIDTH), ops, bsz, l)
    o_s5 = _s5_glu(y_s5, u, s5_d, cast(w_glu), b_glu)

    wb = cast(w_branch)
    merged = _branch_merge(o_attn, o_ssd, o_s5, wb[:Q_W], wb[Q_W:Q_W + SSD_INNER], wb[Q_W + SSD_INNER:], gates)
    return _mm_wres(merged, [(w_out_all, layer)], _ep_plain, n=w_out_all.shape[2], name="out_proj")


def kernel(x, c, positions, w_mod, b_mod, norm_mix_pre, norm_mix_post, norm_ffn_pre, norm_ffn_post, w_in, attn_sinks, conv_w, conv_b, dt_bias, a_log, ssd_d, ssd_norm, s5_lam_re, s5_lam_im, s5_log_step, s5_b_re, s5_b_im, s5_c_re, s5_c_im, s5_d, s5_w_glu, s5_b_glu, w_branch, w_out, ffn_w1, ffn_w3, ffn_w2, w_router, moe_w1, moe_w3, moe_w2):
    bsz, l, d = x.shape
    depth = w_mod.shape[0]
    assert l % SSD_CHUNK == 0 and l % ATTN_BLOCK == 0 and l % S5_CHUNK == 0
    rope = _rope_tables(positions)
    mod = _modulation(c, w_mod, b_mod)
    cast = lambda w: w.astype(BF)

    def mods(i):
        return [mod[i, :, k * d:(k + 1) * d] for k in range(6)]

    sh1, sc1, g1, sh2, sc2, g2 = mods(0)
    h = _pre_norm(x, norm_mix_pre[0], sh1, sc1)
    for i in range(depth):
        m = _hybrid_mixer(h.reshape(bsz * l, d), rope, bsz, l, i, w_in, attn_sinks[i], conv_w[i], conv_b[i],
                          dt_bias[i], a_log[i], ssd_d[i], ssd_norm[i], s5_lam_re[i], s5_lam_im[i],
                          s5_log_step[i], s5_b_re[i], s5_b_im[i], s5_c_re[i], s5_c_im[i], s5_d[i],
                          s5_w_glu[i], s5_b_glu[i], w_branch[i], w_out)
        is_moe = i % 2 == 1
        x, h = _post_norm(m, x, g1, norm_mix_post[i], norm_ffn_pre[i], sh2, sc2, h_dtype=F32 if is_moe else BF)
        h2 = h.reshape(bsz * l, d)
        j = i // 2
        if i % 2 == 0:
            act = _mm_wres(h2, [(ffn_w1, j), (ffn_w3, j)], _ep_swiglu, n=ffn_w1.shape[2], tn=256, tm=2048,
                           name="swiglu_up")
            f = _matmul(act, cast(ffn_w2[j]), tm=512, name="ffn_down")
        else:
            f = _moe(h2, w_router[j], moe_w1[j], moe_w3[j], moe_w2[j])
        if i + 1 < depth:
            sh1, sc1, g1n, sh2n, sc2n, g2n = mods(i + 1)
            x, h = _post_norm(f, x, g2, norm_ffn_post[i], norm_mix_pre[i + 1], sh1, sc1)
            g1, sh2, sc2, g2 = g1n, sh2n, sc2n, g2n
        else:
            x, _ = _post_norm(f, x, g2, norm_ffn_post[i])
    return x
```
